```python
import jax
import jax.numpy as jnp
from jax import lax
import numpy as np

D_MODEL = 1024
BATCH = 4
SEQ = 8192
DEPTH = 1
DEC_BATCH = 8
DEC_SEQ = 64
PAST_LEN = 2048

CHUNK = 64
Q_BLOCK = 128
N_HEADS = 8
HEAD_DIM = 64
ATTN_WIDTH = N_HEADS * HEAD_DIM
N_IDX_HEADS = 8
IDX_DIM = 64
TOPK_MAX = 256
GMLP_GROUPS = 4
GMLP_GROUP_DIM = 128
GMLP_WIDTH = GMLP_GROUPS * GMLP_GROUP_DIM
GMLP_CHUNK = 128
MIX_WIDTH = ATTN_WIDTH + GMLP_WIDTH
IN_SIZES = (ATTN_WIDTH, ATTN_WIDTH, ATTN_WIDTH, N_IDX_HEADS * IDX_DIM, IDX_DIM, N_IDX_HEADS, GMLP_WIDTH, GMLP_WIDTH)
IN_WIDTH = sum(IN_SIZES)
N_MEM = 256
MEM_HEADS = 4
MEM_HEAD_DIM = 128
MEM_WIDTH = MEM_HEADS * MEM_HEAD_DIM
D_FF = 4 * D_MODEL
EPS = 1e-6
LN_EPS = 1e-5

kernel_name = "hybrid_dsa_gmlp_stream_step"


def _split_points():
    return [int(v) for v in np.cumsum(np.array(IN_SIZES))[:-1]]


def rms_norm(x, g):
    xf = x.astype(jnp.float32)
    y = xf * lax.rsqrt(jnp.mean(xf * xf, axis=-1, keepdims=True) + EPS)
    return (y * g.astype(jnp.float32)).astype(x.dtype)


def layer_norm(x, g, b):
    xf = x.astype(jnp.float32)
    mu = jnp.mean(xf, axis=-1, keepdims=True)
    var = jnp.mean(jnp.square(xf - mu), axis=-1, keepdims=True)
    y = (xf - mu) * lax.rsqrt(var + LN_EPS) * g.astype(jnp.float32) + b.astype(jnp.float32)
    return y.astype(x.dtype)


def mix_proj(x, g_pre, w_in, ln_g, ln_b):
    B, T = x.shape[0], x.shape[1]
    h = rms_norm(x, g_pre)
    p = h @ w_in
    q, k, v, iq, ik, iw, zu, zv = jnp.split(p, _split_points(), axis=-1)
    q = q.reshape(B, T, N_HEADS, HEAD_DIM)
    k = k.reshape(B, T, N_HEADS, HEAD_DIM)
    v = v.reshape(B, T, N_HEADS, HEAD_DIM)
    iq = iq.reshape(B, T, N_IDX_HEADS, IDX_DIM)
    iw = iw * (N_IDX_HEADS ** -0.5)
    u = jax.nn.gelu(zu)
    vg = layer_norm(jax.nn.gelu(zv), ln_g, ln_b).reshape(B, T, GMLP_GROUPS, GMLP_GROUP_DIM)
    return q, k, v, iq, ik, iw, u, vg


def dsa_attend(q, iq, iw, lim, k, v, ik, topk):
    S = k.shape[1]
    idx_logits = jnp.einsum('bthe,bse->bths', iq.astype(jnp.float32), ik.astype(jnp.float32)) * (IDX_DIM ** -0.5)
    score = jnp.einsum('bth,bths->bts', iw.astype(jnp.float32), jax.nn.relu(idx_logits))
    admissible = jnp.arange(S, dtype=jnp.int32)[None, :] < lim[:, None]
    score = jnp.where(admissible[None], score, -jnp.inf)
    top_score, top_idx = lax.top_k(score, topk)
    valid = top_score > -jnp.inf
    gather = jax.vmap(lambda a, i: a[i])
    k_sel = gather(k, top_idx)
    v_sel = gather(v, top_idx)
    logits = jnp.einsum('bthd,btkhd->bthk', q.astype(jnp.float32), k_sel.astype(jnp.float32)) * (HEAD_DIM ** -0.5)
    logits = jnp.where(valid[:, :, None, :], logits, -jnp.inf)
    probs = jax.nn.softmax(logits, axis=-1)
    out = jnp.einsum('bthk,btkhd->bthd', probs, v_sel.astype(jnp.float32))
    return out.astype(q.dtype)


def dsa_prompt(q, iq, iw, k, v, ik, topk):
    B, T = q.shape[0], q.shape[1]
    nb = T // Q_BLOCK

    def to_blocks(a):
        return jnp.moveaxis(a.reshape((B, nb, Q_BLOCK) + a.shape[2:]), 1, 0)

    pos = jnp.arange(T, dtype=jnp.int32)
    lim = ((pos // CHUNK + 1) * CHUNK).reshape(nb, Q_BLOCK)

    def body(xs):
        qb, iqb, iwb, limb = xs
        return dsa_attend(qb, iqb, iwb, limb, k, v, ik, topk)

    out = lax.map(body, (to_blocks(q), to_blocks(iq), to_blocks(iw), lim))
    return jnp.moveaxis(out, 0, 1).reshape(q.shape)


def gmlp_mix(v, w_s, b_s):
    n = v.shape[-3]
    tri = jnp.tril(jnp.ones((n, n), dtype=bool))
    w = jnp.where(tri[None], w_s[:, :n, :n], jnp.zeros((), w_s.dtype))
    s = jnp.einsum('gij,...jgd->...igd', w, v)
    return s + jnp.swapaxes(b_s[:, :n], 0, 1)[:, :, None]


def mix_out(attn, u, s, w_o, g_post):
    B, T = attn.shape[0], attn.shape[1]
    cat = jnp.concatenate([attn.reshape(B, T, ATTN_WIDTH), u * s.reshape(B, T, GMLP_WIDTH)], axis=-1)
    return rms_norm(cat @ w_o, g_post)


def mem_kv(mem, g_mem, w_mk, w_mv):
    B = mem.shape[0]
    m = rms_norm(mem, g_mem)
    mk = (m @ w_mk).reshape(B, N_MEM, MEM_HEADS, MEM_HEAD_DIM)
    mv = (m @ w_mv).reshape(B, N_MEM, MEM_HEADS, MEM_HEAD_DIM)
    return mk, mv


def mem_attend(x, mk, mv, g_pre, w_mq, w_mo, g_post):
    B, T = x.shape[0], x.shape[1]
    h = rms_norm(x, g_pre)
    q = (h @ w_mq).reshape(B, T, MEM_HEADS, MEM_HEAD_DIM)
    logits = jnp.einsum('bthd,bmhd->bhtm', q.astype(jnp.float32), mk.astype(jnp.float32)) * (MEM_HEAD_DIM ** -0.5)
    probs = jax.nn.softmax(logits, axis=-1)
    o = jnp.einsum('bhtm,bmhd->bthd', probs, mv.astype(jnp.float32)).astype(x.dtype)
    return rms_norm(o.reshape(B, T, MEM_WIDTH) @ w_mo, g_post)


def ffn(x, g_pre, w1, w2, g_post):
    h = rms_norm(x, g_pre)
    return rms_norm(jnp.square(jax.nn.relu(h @ w1)) @ w2, g_post)


def setup_inputs(seed: int = 0) -> dict:
    key = jax.random.key(seed)
    ks = jax.random.split(key, 32)
    f32 = jnp.float32
    nrm = lambda k, shape, scale: (jax.random.normal(k, shape, f32) * scale)
    gain = lambda k, shape: 1.0 + 0.01 * jax.random.normal(k, shape, f32)
    return {
        "x_prompt": nrm(ks[0], (BATCH, SEQ, D_MODEL), 1.0),
        "x_sample": nrm(ks[1], (DEC_BATCH, DEC_SEQ, D_MODEL), 1.0),
        "cache_attn_k": nrm(ks[2], (DEPTH, DEC_BATCH, PAST_LEN, N_HEADS, HEAD_DIM), 1.0),
        "cache_attn_v": nrm(ks[3], (DEPTH, DEC_BATCH, PAST_LEN, N_HEADS, HEAD_DIM), 1.0),
        "cache_idx_k": nrm(ks[4], (DEPTH, DEC_BATCH, PAST_LEN, IDX_DIM), 1.0),
        "cache_mem_k": nrm(ks[5], (DEPTH, DEC_BATCH, N_MEM, MEM_HEADS, MEM_HEAD_DIM), 1.0),
        "cache_mem_v": nrm(ks[6], (DEPTH, DEC_BATCH, N_MEM, MEM_HEADS, MEM_HEAD_DIM), 1.0),
        "mem_prompt": nrm(ks[7], (BATCH, N_MEM, D_MODEL), 1.0),
        "g_mix_pre": gain(ks[8], (DEPTH, D_MODEL)),
        "w_in": nrm(ks[9], (DEPTH, D_MODEL, IN_WIDTH), D_MODEL ** -0.5),
        "ln_gmlp_g": gain(ks[10], (DEPTH, GMLP_WIDTH)),
        "ln_gmlp_b": nrm(ks[11], (DEPTH, GMLP_WIDTH), 0.02),
        "w_s": nrm(ks[12], (DEPTH, GMLP_GROUPS, GMLP_CHUNK, GMLP_CHUNK), GMLP_CHUNK ** -0.5),
        "b_s": nrm(ks[13], (DEPTH, GMLP_GROUPS, GMLP_CHUNK), 0.1),
        "w_o": nrm(ks[14], (DEPTH, MIX_WIDTH, D_MODEL), MIX_WIDTH ** -0.5),
        "g_mix_post": gain(ks[15], (DEPTH, D_MODEL)),
        "g_mem_kv": gain(ks[16], (DEPTH, D_MODEL)),
        "w_mk": nrm(ks[17], (DEPTH, D_MODEL, MEM_WIDTH), D_MODEL ** -0.5),
        "w_mv": nrm(ks[18], (DEPTH, D_MODEL, MEM_WIDTH), D_MODEL ** -0.5),
        "g_mem_pre": gain(ks[19], (DEPTH, D_MODEL)),
        "w_mq": nrm(ks[20], (DEPTH, D_MODEL, MEM_WIDTH), D_MODEL ** -0.5),
        "w_mo": nrm(ks[21], (DEPTH, MEM_WIDTH, D_MODEL), MEM_WIDTH ** -0.5),
        "g_mem_post": gain(ks[22], (DEPTH, D_MODEL)),
        "g_ffn_pre": gain(ks[23], (DEPTH, D_MODEL)),
        "w1": nrm(ks[24], (DEPTH, D_MODEL, D_FF), D_MODEL ** -0.5),
        "w2": nrm(ks[25], (DEPTH, D_FF, D_MODEL), D_FF ** -0.5),
        "g_ffn_post": gain(ks[26], (DEPTH, D_MODEL)),
    }


def reference(x_prompt, x_sample, cache_attn_k, cache_attn_v, cache_idx_k, cache_mem_k, cache_mem_v, mem_prompt,
              g_mix_pre, w_in, ln_gmlp_g, ln_gmlp_b, w_s, b_s, w_o, g_mix_post, g_mem_kv, w_mk, w_mv,
              g_mem_pre, w_mq, w_mo, g_mem_post, g_ffn_pre, w1, w2, g_ffn_post):
    xp = x_prompt
    xs = x_sample
    Bp, Tp = xp.shape[0], xp.shape[1]
    Ts = xs.shape[1]
    topk_p = min(TOPK_MAX, Tp // 4)
    topk_s = min(TOPK_MAX, (cache_attn_k.shape[2] + Ts) // 4)
    kp_l, vp_l, ikp_l, mkp_l, mvp_l = [], [], [], [], []
    ks_l, vs_l, iks_l, gvs_l = [], [], [], []
    for l in range(DEPTH):
        q, k, v, iq, ik, iw, u, vg = mix_proj(xp, g_mix_pre[l], w_in[l], ln_gmlp_g[l], ln_gmlp_b[l])
        a = dsa_prompt(q, iq, iw, k, v, ik, topk_p)
        s = gmlp_mix(vg.reshape(Bp, Tp // GMLP_CHUNK, GMLP_CHUNK, GMLP_GROUPS, GMLP_GROUP_DIM), w_s[l], b_s[l])
        xp = xp + mix_out(a, u, s.reshape(vg.shape), w_o[l], g_mix_post[l])
        mk, mv = mem_kv(mem_prompt, g_mem_kv[l], w_mk[l], w_mv[l])
        xp = xp + mem_attend(xp, mk, mv, g_mem_pre[l], w_mq[l], w_mo[l], g_mem_post[l])
        xp = xp + ffn(xp, g_ffn_pre[l], w1[l], w2[l], g_ffn_post[l])
        kp_l.append(k)
        vp_l.append(v)
        ikp_l.append(ik)
        mkp_l.append(mk)
        mvp_l.append(mv)
        q, k, v, iq, ik, iw, u, vg = mix_proj(xs, g_mix_pre[l], w_in[l], ln_gmlp_g[l], ln_gmlp_b[l])
        k_all = jnp.concatenate([cache_attn_k[l], k], axis=1)
        v_all = jnp.concatenate([cache_attn_v[l], v], axis=1)
        ik_all = jnp.concatenate([cache_idx_k[l], ik], axis=1)
        lim = jnp.full((Ts,), k_all.shape[1], dtype=jnp.int32)
        a = dsa_attend(q, iq, iw, lim, k_all, v_all, ik_all, topk_s)
        s = gmlp_mix(vg, w_s[l], b_s[l])
        xs = xs + mix_out(a, u, s, w_o[l], g_mix_post[l])
        xs = xs + mem_attend(xs, cache_mem_k[l], cache_mem_v[l], g_mem_pre[l], w_mq[l], w_mo[l], g_mem_post[l])
        xs = xs + ffn(xs, g_ffn_pre[l], w1[l], w2[l], g_ffn_post[l])
        ks_l.append(k)
        vs_l.append(v)
        iks_l.append(ik)
        gvs_l.append(vg)
    return (xp, xs, jnp.stack(kp_l), jnp.stack(vp_l), jnp.stack(ikp_l), jnp.stack(mkp_l), jnp.stack(mvp_l),
            jnp.stack(ks_l), jnp.stack(vs_l), jnp.stack(iks_l), jnp.stack(gvs_l))
```

```python
import functools

import jax
import jax.numpy as jnp
import numpy as np
from jax import lax
from jax.experimental import pallas as pl
from jax.experimental.pallas import tpu as pltpu

F32 = jnp.float32
BF16 = jnp.bfloat16

D_MODEL = 1024
N_HEADS = 8
HEAD_DIM = 64
ATTN_WIDTH = N_HEADS * HEAD_DIM
N_IDX_HEADS = 8
IDX_DIM = 64
GMLP_GROUPS = 4
GMLP_GROUP_DIM = 128
GMLP_WIDTH = GMLP_GROUPS * GMLP_GROUP_DIM
GMLP_CHUNK = 128
CHUNK = 64
TOPK_MAX = 256
MEM_HEADS = 4
MEM_HEAD_DIM = 128
MEM_WIDTH = MEM_HEADS * MEM_HEAD_DIM
EPS = 1e-6
LN_EPS = 1e-5

LANES = 128
SUBLANES = 8
VMEM_LIMIT = 56 * 1024 * 1024

_Q0, _K0, _V0, _IQ0, _IKW0, _ZU0, _ZV0, _WPAD = 0, 512, 1024, 1536, 2048, 2176, 2688, 3200
_IW_LO, _IW_HI = IDX_DIM, IDX_DIM + N_IDX_HEADS

NQ = 128
KT = 256


def _rms(x, g):
    return x * lax.rsqrt(jnp.mean(x * x, axis=-1, keepdims=True) + EPS) * g


def _params(n_axes):
    return pltpu.CompilerParams(dimension_semantics=("arbitrary",) * n_axes, vmem_limit_bytes=VMEM_LIMIT)


def _const_spec(shape):
    return pl.BlockSpec(shape, lambda *_: (0,) * len(shape), pipeline_mode=pl.Buffered(1))


def _proj_kernel(x_ref, g_ref, w_ref, lng_ref, lnb_ref,
                 q_ref, k_ref, kb_ref, v_ref, vb_ref, iq_ref, ikw_ref, u_ref, vg_ref):
    h = _rms(x_ref[...], g_ref[...]).astype(BF16)

    def seg(lo, hi):
        return jnp.dot(h, w_ref[:, lo:hi], preferred_element_type=F32)

    q_ref[...] = (seg(_Q0, _K0) * (HEAD_DIM ** -0.5)).astype(BF16)
    k = seg(_K0, _V0)
    k_ref[...] = k
    kb_ref[...] = k.astype(BF16)
    v = seg(_V0, _IQ0)
    v_ref[...] = v
    vb_ref[...] = v.astype(BF16)
    iq_ref[...] = (seg(_IQ0, _IKW0) * (IDX_DIM ** -0.5)).astype(BF16)
    ikw = seg(_IKW0, _ZU0)
    lane = lax.broadcasted_iota(jnp.int32, ikw.shape, 1)
    is_iw = (lane >= _IW_LO) & (lane < _IW_HI)
    ikw_ref[...] = jnp.where(is_iw, ikw * (N_IDX_HEADS ** -0.5), ikw)
    u_ref[...] = jax.nn.gelu(seg(_ZU0, _ZV0))
    gv = jax.nn.gelu(seg(_ZV0, _WPAD))
    mu = jnp.mean(gv, axis=-1, keepdims=True)
    var = jnp.mean(jnp.square(gv - mu), axis=-1, keepdims=True)
    vg_ref[...] = (gv - mu) * lax.rsqrt(var + LN_EPS) * lng_ref[...] + lnb_ref[...]


def _proj(x2d, g_pre, w_pad, ln_g, ln_b):
    n = x2d.shape[0]
    tm = min(512, n)
    tok = lambda w: pl.BlockSpec((tm, w), lambda i: (i, 0))
    outs = [(ATTN_WIDTH, BF16), (ATTN_WIDTH, F32), (ATTN_WIDTH, BF16), (ATTN_WIDTH, F32), (ATTN_WIDTH, BF16),
            (ATTN_WIDTH, BF16), (LANES, F32), (GMLP_WIDTH, F32), (GMLP_WIDTH, F32)]
    return pl.pallas_call(
        _proj_kernel,
        grid=(n // tm,),
        in_specs=[tok(D_MODEL), _const_spec((1, D_MODEL)), _const_spec((D_MODEL, _WPAD)),
                  _const_spec((1, GMLP_WIDTH)), _const_spec((1, GMLP_WIDTH))],
        out_specs=[tok(w) for w, _ in outs],
        out_shape=[jax.ShapeDtypeStruct((n, w), dt) for w, dt in outs],
        compiler_params=_params(1),
        name="proj",
    )(x2d, g_pre, w_pad, ln_g, ln_b)


def _key_to_float(u):
    bits = jnp.where(u < 0, u ^ jnp.int32(-2 ** 31), ~u)
    return lax.bitcast_convert_type(bits, F32)


def _dsa_kernel(qT_ref, iqT_ref, iwT_ref, ik_ref, k_ref, vT_ref, o_ref,
                score_ref, widx_ref, wq_ref, acc_ref, m_ref, l_ref,
                *, causal, n_keys, s_pad, nq_valid, topk):
    j = pl.program_id(1)
    lane = lax.broadcasted_iota(jnp.int32, (1, NQ), 1)
    if causal:
        pos = j * NQ + lane
        lim = (lax.shift_right_logical(pos, CHUNK.bit_length() - 1) + 1) * CHUNK
        nt = ((j + 1) * NQ + (KT - 1)) // KT
    else:
        lim = jnp.full((1, NQ), n_keys, jnp.int32)
        nt = s_pad // KT
    neg_inf = jnp.float32(-jnp.inf)
    row_iota = lax.broadcasted_iota(jnp.int32, (KT, NQ), 0)

    def fold(x, op):
        return op(op(x.reshape(KT // SUBLANES, SUBLANES, NQ), axis=0), axis=0, keepdims=True)

    for h in range(N_IDX_HEADS):
        widx_ref[:, h * NQ:(h + 1) * NQ] = iqT_ref[h * IDX_DIM:(h + 1) * IDX_DIM, :]
    pair_row = lax.broadcasted_iota(jnp.int32, (2 * HEAD_DIM, NQ), 0)
    for p in range(N_HEADS // 2):
        blk = qT_ref[p * 2 * HEAD_DIM:(p + 1) * 2 * HEAD_DIM, :]
        zero = jnp.zeros_like(blk)
        wq_ref[p, :, 0:NQ] = jnp.where(pair_row < HEAD_DIM, blk, zero)
        wq_ref[p, :, NQ:2 * NQ] = jnp.where(pair_row >= HEAD_DIM, blk, zero)

    def idx_body(t, carry):
        r0 = pl.multiple_of(t * KT, KT)
        logits = jnp.dot(ik_ref[pl.ds(r0, KT), :], widx_ref[...], preferred_element_type=F32)
        sc = jnp.zeros((KT, NQ), F32)
        for h in range(N_IDX_HEADS):
            sc = sc + jnp.maximum(logits[:, h * NQ:(h + 1) * NQ], 0.0) * iwT_ref[h:h + 1, :]
        score_ref[pl.ds(r0, KT), :] = jnp.where(r0 + row_iota < lim, sc, neg_inf)
        return carry

    lax.fori_loop(0, nt, idx_body, 0)

    def count(indicator):
        def body(t, acc):
            r0 = pl.multiple_of(t * KT, KT)
            ind = indicator(score_ref[pl.ds(r0, KT), :], r0)
            return acc + jnp.sum(ind.reshape(KT // SUBLANES, SUBLANES, NQ), axis=0)
        acc = lax.fori_loop(0, nt, body, jnp.zeros((SUBLANES, NQ), F32))
        return jnp.sum(acc, axis=0, keepdims=True)

    def bit_body(i, prefix):
        trial = prefix | lax.shift_left(jnp.int32(1), 31 - i)
        cand = _key_to_float(trial)
        c = count(lambda x, r0: jnp.where(x < cand, 0.0, 1.0))
        return jnp.where(c >= topk, trial, prefix)

    prefix = lax.fori_loop(0, 32, bit_body, jnp.zeros((1, NQ), jnp.int32))
    thr = jnp.maximum(_key_to_float(prefix), jnp.finfo(F32).min)
    n_gt = count(lambda x, r0: jnp.where(x > thr, 1.0, 0.0))
    n_eq = count(lambda x, r0: jnp.where(x == thr, 1.0, 0.0))
    need = topk - n_gt
    tie = jnp.where((n_eq > need) & (lane < nq_valid), 1.0, 0.0)
    has_tie = jnp.max(tie) > 0.0

    @pl.when(jnp.logical_not(has_tie))
    def _():
        def body(t, carry):
            r0 = pl.multiple_of(t * KT, KT)
            x = score_ref[pl.ds(r0, KT), :]
            score_ref[pl.ds(r0, KT), :] = jnp.where(x >= thr, 0.0, neg_inf)
            return carry
        lax.fori_loop(0, nt, body, 0)

    @pl.when(has_tie)
    def _():
        n_bits = int(s_pad).bit_length()

        def idx_bit(i, cut):
            trial = cut | lax.shift_left(jnp.int32(1), n_bits - 1 - i)
            c = count(lambda x, r0: jnp.where(x == thr, jnp.where(r0 + row_iota < trial, 1.0, 0.0), 0.0))
            return jnp.where(c <= need, trial, cut)

        cut = lax.fori_loop(0, n_bits, idx_bit, jnp.zeros((1, NQ), jnp.int32))

        def body(t, carry):
            r0 = pl.multiple_of(t * KT, KT)
            x = score_ref[pl.ds(r0, KT), :]
            tied = jnp.where(r0 + row_iota < cut, 0.0, neg_inf)
            score_ref[pl.ds(r0, KT), :] = jnp.where(x > thr, 0.0, jnp.where(x == thr, tied, neg_inf))
            return carry
        lax.fori_loop(0, nt, body, 0)

    m_ref[...] = jnp.full(m_ref.shape, neg_inf, F32)
    l_ref[...] = jnp.zeros(l_ref.shape, F32)
    acc_ref[...] = jnp.zeros(acc_ref.shape, F32)

    def att_body(t, carry):
        r0 = pl.multiple_of(t * KT, KT)
        bias = score_ref[pl.ds(r0, KT), :]
        for p in range(N_HEADS // 2):
            lp = jnp.dot(k_ref[pl.ds(r0, KT), p * 2 * HEAD_DIM:(p + 1) * 2 * HEAD_DIM], wq_ref[p],
                         preferred_element_type=F32)
            for hh in range(2):
                h = 2 * p + hh
                rows = slice(h * HEAD_DIM, (h + 1) * HEAD_DIM)
                s = lp[:, hh * NQ:(hh + 1) * NQ] + bias
                m_old = m_ref[h:h + 1, :]
                m_new = jnp.maximum(m_old, fold(s, jnp.max))
                m_safe = jnp.where(m_new == neg_inf, 0.0, m_new)
                pr = jnp.exp(s - m_safe)
                alpha = jnp.exp(m_old - m_safe)
                l_ref[h:h + 1, :] = alpha * l_ref[h:h + 1, :] + fold(pr, jnp.sum)
                pv = jnp.dot(vT_ref[rows, pl.ds(r0, KT)], pr.astype(BF16), preferred_element_type=F32)
                acc_ref[rows, :] = alpha * acc_ref[rows, :] + pv
                m_ref[h:h + 1, :] = m_new
        return carry

    lax.fori_loop(0, nt, att_body, 0)

    for h in range(N_HEADS):
        rows = slice(h * HEAD_DIM, (h + 1) * HEAD_DIM)
        acc_ref[rows, :] = acc_ref[rows, :] * (1.0 / l_ref[h:h + 1, :])
    o_ref[...] = acc_ref[...].T


def _dsa(qT, iqT, iwT, ikb, kb, vT, *, causal, n_keys, nq_valid, topk):
    b, _, t = qT.shape
    s_pad = kb.shape[1]
    kern = functools.partial(_dsa_kernel, causal=causal, n_keys=n_keys, s_pad=s_pad, nq_valid=nq_valid, topk=topk)
    per_q = lambda rows: pl.BlockSpec((None, rows, NQ), lambda bi, j: (bi, 0, j))
    per_b = lambda r, c: pl.BlockSpec((None, r, c), lambda bi, j: (bi, 0, 0), pipeline_mode=pl.Buffered(1))
    return pl.pallas_call(
        kern,
        grid=(b, t // NQ),
        in_specs=[per_q(ATTN_WIDTH), per_q(N_IDX_HEADS * IDX_DIM), per_q(N_IDX_HEADS),
                  per_b(s_pad, IDX_DIM), per_b(s_pad, ATTN_WIDTH), per_b(ATTN_WIDTH, s_pad)],
        out_specs=pl.BlockSpec((None, NQ, ATTN_WIDTH), lambda bi, j: (bi, j, 0)),
        out_shape=jax.ShapeDtypeStruct((b, t, ATTN_WIDTH), F32),
        scratch_shapes=[pltpu.VMEM((s_pad, NQ), F32),
                        pltpu.VMEM((IDX_DIM, N_IDX_HEADS * NQ), BF16),
                        pltpu.VMEM((N_HEADS // 2, 2 * HEAD_DIM, 2 * NQ), BF16),
                        pltpu.VMEM((ATTN_WIDTH, NQ), F32),
                        pltpu.VMEM((N_HEADS, NQ), F32),
                        pltpu.VMEM((N_HEADS, NQ), F32)],
        compiler_params=_params(2),
        name="dsa",
    )(qT, iqT, iwT, ikb, kb, vT)


def _mix_out_kernel(x_ref, a_ref, u_ref, vg_ref, ws_ref, bsT_ref, wo_ref, g_ref, y_ref, gated_ref, *, n):
    tm = x_ref.shape[0]
    r = lax.broadcasted_iota(jnp.int32, (n, n), 0)
    c = lax.broadcasted_iota(jnp.int32, (n, n), 1)
    for g in range(GMLP_GROUPS):
        cols = slice(g * GMLP_GROUP_DIM, (g + 1) * GMLP_GROUP_DIM)
        w = jnp.where(c <= r, ws_ref[g], 0.0).astype(BF16)
        bias = bsT_ref[:, g:g + 1]
        for ch in range(tm // n):
            rows = slice(ch * n, (ch + 1) * n)
            s = jnp.dot(w, vg_ref[rows, cols].astype(BF16), preferred_element_type=F32) + bias
            gated_ref[rows, cols] = u_ref[rows, cols] * s
    mix = jnp.dot(a_ref[...].astype(BF16), wo_ref[:ATTN_WIDTH, :], preferred_element_type=F32)
    mix = mix + jnp.dot(gated_ref[...].astype(BF16), wo_ref[ATTN_WIDTH:, :], preferred_element_type=F32)
    y_ref[...] = x_ref[...] + _rms(mix, g_ref[...])


def _mix_out(x2d, attn, u, vg, ws_n, bsT_n, wo_b, g_post, *, n, tm):
    ntok = x2d.shape[0]
    tok = lambda w: pl.BlockSpec((tm, w), lambda i: (i, 0))
    return pl.pallas_call(
        functools.partial(_mix_out_kernel, n=n),
        grid=(ntok // tm,),
        in_specs=[tok(D_MODEL), tok(ATTN_WIDTH), tok(GMLP_WIDTH), tok(GMLP_WIDTH),
                  _const_spec((GMLP_GROUPS, n, n)), _const_spec((n, GMLP_GROUPS)),
                  _const_spec((ATTN_WIDTH + GMLP_WIDTH, D_MODEL)), _const_spec((1, D_MODEL))],
        out_specs=tok(D_MODEL),
        out_shape=jax.ShapeDtypeStruct((ntok, D_MODEL), F32),
        scratch_shapes=[pltpu.VMEM((tm, GMLP_WIDTH), F32)],
        compiler_params=_params(1),
        name="mix_out",
    )(x2d, attn, u, vg, ws_n, bsT_n, wo_b, g_post)


def _mem_kv_kernel(m_ref, g_ref, wk_ref, wv_ref, mk_ref, mv_ref):
    h = _rms(m_ref[...], g_ref[...]).astype(BF16)
    mk_ref[...] = jnp.dot(h, wk_ref[...], preferred_element_type=F32)
    mv_ref[...] = jnp.dot(h, wv_ref[...], preferred_element_type=F32)


def _mem_kv(mem2d, g, wk_b, wv_b):
    n = mem2d.shape[0]
    tm = min(512, n)
    tok = lambda w: pl.BlockSpec((tm, w), lambda i: (i, 0))
    return pl.pallas_call(
        _mem_kv_kernel,
        grid=(n // tm,),
        in_specs=[tok(D_MODEL), _const_spec((1, D_MODEL)), _const_spec((D_MODEL, MEM_WIDTH)),
                  _const_spec((D_MODEL, MEM_WIDTH))],
        out_specs=[tok(MEM_WIDTH), tok(MEM_WIDTH)],
        out_shape=[jax.ShapeDtypeStruct((n, MEM_WIDTH), F32)] * 2,
        compiler_params=_params(1),
        name="mem_kv",
    )(mem2d, g, wk_b, wv_b)


def _mem_attend_kernel(x_ref, mkT_ref, mv_ref, gpre_ref, wq_ref, wo_ref, gpost_ref, y_ref, o_ref):
    x = x_ref[...]
    h = _rms(x, gpre_ref[...]).astype(BF16)
    q = jnp.dot(h, wq_ref[...], preferred_element_type=F32).astype(BF16)
    for hd in range(MEM_HEADS):
        cols = slice(hd * MEM_HEAD_DIM, (hd + 1) * MEM_HEAD_DIM)
        lg = jnp.dot(q[:, cols], mkT_ref[cols, :], preferred_element_type=F32) * (MEM_HEAD_DIM ** -0.5)
        e = jnp.exp(lg - jnp.max(lg, axis=-1, keepdims=True))
        p = e / jnp.sum(e, axis=-1, keepdims=True)
        o_ref[:, cols] = jnp.dot(p.astype(BF16), mv_ref[:, cols], preferred_element_type=F32)
    out = jnp.dot(o_ref[...].astype(BF16), wo_ref[...], preferred_element_type=F32)
    y_ref[...] = x + _rms(out, gpost_ref[...])


def _mem_attend(x2d, mkT, mvb, g_pre, wq_b, wo_b, g_post, *, tm, tiles_per_batch):
    ntok = x2d.shape[0]
    n_mem = mvb.shape[1]
    tok = lambda w: pl.BlockSpec((tm, w), lambda i: (i, 0))
    return pl.pallas_call(
        _mem_attend_kernel,
        grid=(ntok // tm,),
        in_specs=[tok(D_MODEL),
                  pl.BlockSpec((None, MEM_WIDTH, n_mem), lambda i: (i // tiles_per_batch, 0, 0)),
                  pl.BlockSpec((None, n_mem, MEM_WIDTH), lambda i: (i // tiles_per_batch, 0, 0)),
                  _const_spec((1, D_MODEL)), _const_spec((D_MODEL, MEM_WIDTH)),
                  _const_spec((MEM_WIDTH, D_MODEL)), _const_spec((1, D_MODEL))],
        out_specs=tok(D_MODEL),
        out_shape=jax.ShapeDtypeStruct((ntok, D_MODEL), F32),
        scratch_shapes=[pltpu.VMEM((tm, MEM_WIDTH), F32)],
        compiler_params=_params(1),
        name="mem_attend",
    )(x2d, mkT, mvb, g_pre, wq_b, wo_b, g_post)


FF_CHUNK = 1024


def _ffn_kernel(x_ref, gpre_ref, w1_ref, w2_ref, gpost_ref, y_ref):
    x = x_ref[...]
    h = _rms(x, gpre_ref[...]).astype(BF16)
    d_ff = w1_ref.shape[1]
    out = jnp.zeros(x.shape, F32)
    for c in range(d_ff // FF_CHUNK):
        cols = slice(c * FF_CHUNK, (c + 1) * FF_CHUNK)
        a = jnp.maximum(jnp.dot(h, w1_ref[:, cols], preferred_element_type=F32), 0.0)
        out = out + jnp.dot((a * a).astype(BF16), w2_ref[cols, :], preferred_element_type=F32)
    y_ref[...] = x + _rms(out, gpost_ref[...])


def _ffn(x2d, g_pre, w1_b, w2_b, g_post):
    ntok = x2d.shape[0]
    tm = min(512, ntok)
    d_ff = w1_b.shape[1]
    tok = pl.BlockSpec((tm, D_MODEL), lambda i: (i, 0))
    return pl.pallas_call(
        _ffn_kernel,
        grid=(ntok // tm,),
        in_specs=[tok, _const_spec((1, D_MODEL)), _const_spec((D_MODEL, d_ff)), _const_spec((d_ff, D_MODEL)),
                  _const_spec((1, D_MODEL))],
        out_specs=tok,
        out_shape=jax.ShapeDtypeStruct((ntok, D_MODEL), F32),
        compiler_params=_params(1),
        name="ffn",
    )(x2d, g_pre, w1_b, w2_b, g_post)


def _pad_axis(a, axis, size):
    pad = [(0, 0)] * a.ndim
    pad[axis] = (0, size - a.shape[axis])
    return jnp.pad(a, pad)


def _group(x, w, *, past=None, mem_kv=None):
    b, t, _ = x.shape
    x2d = x.reshape(b * t, D_MODEL)
    q, k, kb, v, vb, iq, ikw, u, vg = _proj(x2d, w["g_mix_pre"], w["w_pad"], w["ln_g"], w["ln_b"])
    ik = ikw[:, :IDX_DIM]
    per_stream = lambda a: a.reshape(b, t, a.shape[-1])
    kb3, vb3, ikb3 = per_stream(kb), per_stream(vb), per_stream(ik.astype(BF16))
    qT = jnp.swapaxes(per_stream(q), 1, 2)
    iqT = jnp.swapaxes(per_stream(iq), 1, 2)
    iwT = jnp.swapaxes(per_stream(ikw[:, _IW_LO:_IW_HI]), 1, 2)
    if past is None:
        n_keys, causal, tq = t, True, t
    else:
        ck, cv, cik = past
        kb3 = jnp.concatenate([ck.astype(BF16), kb3], axis=1)
        vb3 = jnp.concatenate([cv.astype(BF16), vb3], axis=1)
        ikb3 = jnp.concatenate([cik.astype(BF16), ikb3], axis=1)
        n_keys, causal = kb3.shape[1], False
        s_pad = -(-n_keys // KT) * KT
        kb3, vb3, ikb3 = (_pad_axis(a, 1, s_pad) for a in (kb3, vb3, ikb3))
        tq = -(-t // NQ) * NQ
        qT, iqT, iwT = (_pad_axis(a, 2, tq) for a in (qT, iqT, iwT))
    topk = min(TOPK_MAX, n_keys // 4)
    attn = _dsa(qT, iqT, iwT, ikb3, kb3, jnp.swapaxes(vb3, 1, 2),
                causal=causal, n_keys=n_keys, nq_valid=min(t, NQ), topk=topk)
    attn2d = attn[:, :t].reshape(b * t, ATTN_WIDTH)

    n = min(t, GMLP_CHUNK)
    tm = min(512, b * t)
    x2d = _mix_out(x2d, attn2d, u, vg, w["w_s"][:, :n, :n], jnp.swapaxes(w["b_s"][:, :n], 0, 1),
                   w["w_o"], w["g_mix_post"], n=n, tm=tm)
    mk, mv = mem_kv
    tm_mem = min(512, t)
    x2d = _mem_attend(x2d, jnp.swapaxes(mk, 1, 2).astype(BF16), mv.astype(BF16), w["g_mem_pre"], w["w_mq"],
                      w["w_mo"], w["g_mem_post"], tm=tm_mem, tiles_per_batch=t // tm_mem)
    x2d = _ffn(x2d, w["g_ffn_pre"], w["w1"], w["w2"], w["g_ffn_post"])
    new = (k.reshape(b, t, N_HEADS, HEAD_DIM), v.reshape(b, t, N_HEADS, HEAD_DIM), ik.reshape(b, t, IDX_DIM),
           vg.reshape(b, t, GMLP_GROUPS, GMLP_GROUP_DIM))
    return x2d.reshape(b, t, D_MODEL), new


def kernel(x_prompt, x_sample, cache_attn_k, cache_attn_v, cache_idx_k, cache_mem_k, cache_mem_v, mem_prompt, g_mix_pre, w_in, ln_gmlp_g, ln_gmlp_b, w_s, b_s, w_o, g_mix_post, g_mem_kv, w_mk, w_mv, g_mem_pre, w_mq, w_mo, g_mem_post, g_ffn_pre, w1, w2, g_ffn_post):
    depth = w_in.shape[0]
    xp, xs = x_prompt, x_sample
    bp, n_mem = mem_prompt.shape[0], mem_prompt.shape[1]
    bs = xs.shape[0]
    row = lambda a: a.reshape(1, -1)
    outs = [[] for _ in range(9)]
    for l in range(depth):
        split = _IKW0 + _IW_HI
        w_pad = jnp.concatenate([w_in[l][:, :split], jnp.zeros((D_MODEL, _ZU0 - split), F32), w_in[l][:, split:]],
                                axis=1).astype(BF16)
        w = dict(g_mix_pre=row(g_mix_pre[l]), w_pad=w_pad, ln_g=row(ln_gmlp_g[l]), ln_b=row(ln_gmlp_b[l]),
                 w_s=w_s[l], b_s=b_s[l], w_o=w_o[l].astype(BF16), g_mix_post=row(g_mix_post[l]),
                 g_mem_pre=row(g_mem_pre[l]), w_mq=w_mq[l].astype(BF16), w_mo=w_mo[l].astype(BF16),
                 g_mem_post=row(g_mem_post[l]), g_ffn_pre=row(g_ffn_pre[l]), w1=w1[l].astype(BF16),
                 w2=w2[l].astype(BF16), g_ffn_post=row(g_ffn_post[l]))
        mk, mv = _mem_kv(mem_prompt.reshape(bp * n_mem, D_MODEL), row(g_mem_kv[l]), w_mk[l].astype(BF16),
                         w_mv[l].astype(BF16))
        mk, mv = mk.reshape(bp, n_mem, MEM_WIDTH), mv.reshape(bp, n_mem, MEM_WIDTH)
        xp, (kp, vp, ikp, _) = _group(xp, w, mem_kv=(mk, mv))
        past = (cache_attn_k[l].reshape(bs, -1, ATTN_WIDTH), cache_attn_v[l].reshape(bs, -1, ATTN_WIDTH),
                cache_idx_k[l])
        cmk = cache_mem_k[l].reshape(bs, -1, MEM_WIDTH)
        cmv = cache_mem_v[l].reshape(bs, -1, MEM_WIDTH)
        xs, (ks, vs, iks, gvs) = _group(xs, w, past=past, mem_kv=(cmk, cmv))
        new = (kp, vp, ikp, mk.reshape(bp, n_mem, MEM_HEADS, MEM_HEAD_DIM),
               mv.reshape(bp, n_mem, MEM_HEADS, MEM_HEAD_DIM), ks, vs, iks, gvs)
        for lst, a in zip(outs, new):
            lst.append(a)
    return (xp, xs) + tuple(jnp.stack(lst) for lst in outs)
```

```python
import functools
import math

import jax
import jax.numpy as jnp
from jax import lax
from jax.experimental import pallas as pl
from jax.experimental.pallas import tpu as pltpu

F32 = jnp.float32
BF16 = jnp.bfloat16

D_MODEL = 1024
N_HEADS = 8
HEAD_DIM = 64
ATTN_WIDTH = N_HEADS * HEAD_DIM
N_IDX_HEADS = 8
IDX_DIM = 64
GMLP_GROUPS = 4
GMLP_GROUP_DIM = 128
GMLP_WIDTH = GMLP_GROUPS * GMLP_GROUP_DIM
GMLP_CHUNK = 128
CHUNK = 64
TOPK_MAX = 256
MEM_HEADS = 4
MEM_HEAD_DIM = 128
MEM_WIDTH = MEM_HEADS * MEM_HEAD_DIM
EPS = 1e-6
LN_EPS = 1e-5

LANES = 128
SUBLANES = 8
VMEM_LIMIT = 56 * 1024 * 1024

_Q0, _K0, _V0, _IQ0, _IKW0, _ZU0, _ZV0, _WPAD = 0, 512, 1024, 1536, 2048, 2176, 2688, 3200
_IW_LO, _IW_HI = IDX_DIM, IDX_DIM + N_IDX_HEADS

NQ = 128
KT = 256
ST = 512
V_ROWS = 80
ACC_CHAINS = 4
FINE_BITS = 17
FINE_SPAN = 2 ** FINE_BITS


def _rms(x, g):
    return x * lax.rsqrt(jnp.mean(x * x, axis=-1, keepdims=True) + EPS) * g


def _params(n_axes):
    return pltpu.CompilerParams(dimension_semantics=("arbitrary",) * n_axes, vmem_limit_bytes=VMEM_LIMIT)


def _const_spec(shape):
    return pl.BlockSpec(shape, lambda *_: (0,) * len(shape), pipeline_mode=pl.Buffered(1))


def _proj_kernel(x_ref, g_ref, w_ref, lng_ref, lnb_ref,
                 q_ref, k_ref, kb_ref, v_ref, vb_ref, iq_ref, ikw_ref, u_ref, vg_ref):
    h = _rms(x_ref[...], g_ref[...]).astype(BF16)

    def seg(lo, hi):
        return jnp.dot(h, w_ref[:, lo:hi], preferred_element_type=F32)

    q_ref[...] = (seg(_Q0, _K0) * (HEAD_DIM ** -0.5 * math.log2(math.e))).astype(BF16)
    k = seg(_K0, _V0)
    k_ref[...] = k
    kb_ref[...] = k.astype(BF16)
    v = seg(_V0, _IQ0)
    v_ref[...] = v
    vb_ref[...] = v.astype(BF16)
    iq_ref[...] = (seg(_IQ0, _IKW0) * (IDX_DIM ** -0.5)).astype(BF16)
    ikw = seg(_IKW0, _ZU0)
    lane = lax.broadcasted_iota(jnp.int32, ikw.shape, 1)
    is_iw = (lane >= _IW_LO) & (lane < _IW_HI)
    ikw_ref[...] = jnp.where(is_iw, ikw * (N_IDX_HEADS ** -0.5), ikw)
    u_ref[...] = jax.nn.gelu(seg(_ZU0, _ZV0))
    gv = jax.nn.gelu(seg(_ZV0, _WPAD))
    mu = jnp.mean(gv, axis=-1, keepdims=True)
    var = jnp.mean(jnp.square(gv - mu), axis=-1, keepdims=True)
    vg_ref[...] = (gv - mu) * lax.rsqrt(var + LN_EPS) * lng_ref[...] + lnb_ref[...]


def _proj(x2d, g_pre, w_pad, ln_g, ln_b):
    n = x2d.shape[0]
    tm = min(512, n)
    tok = lambda w: pl.BlockSpec((tm, w), lambda i: (i, 0))
    outs = [(ATTN_WIDTH, BF16), (ATTN_WIDTH, F32), (ATTN_WIDTH, BF16), (ATTN_WIDTH, F32), (ATTN_WIDTH, BF16),
            (ATTN_WIDTH, BF16), (LANES, F32), (GMLP_WIDTH, F32), (GMLP_WIDTH, F32)]
    return pl.pallas_call(
        _proj_kernel,
        grid=(n // tm,),
        in_specs=[tok(D_MODEL), _const_spec((1, D_MODEL)), _const_spec((D_MODEL, _WPAD)),
                  _const_spec((1, GMLP_WIDTH)), _const_spec((1, GMLP_WIDTH))],
        out_specs=[tok(w) for w, _ in outs],
        out_shape=[jax.ShapeDtypeStruct((n, w), dt) for w, dt in outs],
        compiler_params=_params(1),
        name="proj",
    )(x2d, g_pre, w_pad, ln_g, ln_b)


def _key_to_float(u):
    bits = jnp.where(u < 0, u ^ jnp.int32(-2 ** 31), ~u)
    return lax.bitcast_convert_type(bits, F32)


def _fold(x, op):
    part = op(x.reshape(-1, ACC_CHAINS * SUBLANES, NQ), axis=0)
    return op(part.reshape(ACC_CHAINS, SUBLANES, NQ), axis=0)


def _dsa_kernel(qT_ref, iqT_ref, iwT_ref, ik_ref, k_ref, vT_ref, o_ref,
                score_ref, rscore_ref, widx_ref, wq_ref, s_ref, acc_ref, outT_ref, mt_ref, tie_ref,
                *, causal, n_keys, s_pad, nq_valid, topk):
    j = pl.program_id(1)
    lane = lax.broadcasted_iota(jnp.int32, (1, NQ), 1)
    if causal:
        pos = j * NQ + lane
        lim = (lax.shift_right_logical(pos, CHUNK.bit_length() - 1) + 1) * CHUNK
        nt = ((j + 1) * NQ + (ST - 1)) // ST
        nt_att = ((nt + 1) // 2) * 2
    else:
        lim = jnp.full((1, NQ), n_keys, jnp.int32)
        nt = nt_att = s_pad // ST
    neg_inf = jnp.float32(-jnp.inf)
    row_iota = lax.broadcasted_iota(jnp.int32, (ST, NQ), 0)

    def tile(t):
        return pl.ds(pl.multiple_of(t * ST, ST), ST)

    for h in range(N_IDX_HEADS):
        widx_ref[:, h * NQ:(h + 1) * NQ] = iqT_ref[h * IDX_DIM:(h + 1) * IDX_DIM, :]
    pair_row = lax.broadcasted_iota(jnp.int32, (2 * HEAD_DIM, NQ), 0)
    for p in range(N_HEADS // 2):
        blk = qT_ref[p * 2 * HEAD_DIM:(p + 1) * 2 * HEAD_DIM, :]
        zero = jnp.zeros_like(blk)
        wq_ref[p, :, 0:NQ] = jnp.where(pair_row < HEAD_DIM, blk, zero)
        wq_ref[p, :, NQ:2 * NQ] = jnp.where(pair_row >= HEAD_DIM, blk, zero)

    def idx_body(t, carry):
        for sub in range(ST // KT):
            r0 = pl.multiple_of(t * ST + sub * KT, KT)
            logits = jnp.dot(ik_ref[pl.ds(r0, KT), :], widx_ref[...], preferred_element_type=F32)
            sc = jnp.zeros((KT, NQ), F32)
            for h in range(N_IDX_HEADS):
                sc = sc + jnp.maximum(logits[:, h * NQ:(h + 1) * NQ], 0.0) * iwT_ref[h:h + 1, :]
            sc = jnp.where(r0 + row_iota[:KT] < lim, sc, neg_inf)
            score_ref[pl.ds(r0, KT), :] = sc
            rscore_ref[pl.ds(r0, KT), :] = sc.astype(BF16)
        return carry

    lax.fori_loop(0, nt, idx_body, 0)

    def count(indicator):
        def body(t, acc):
            return acc + _fold(indicator(score_ref[tile(t), :], t * ST), jnp.sum)
        acc = lax.fori_loop(0, nt, body, jnp.zeros((SUBLANES, NQ), F32))
        return jnp.sum(acc, axis=0, keepdims=True)

    def count_rounded(cand):
        cb = jnp.broadcast_to(cand, (2 * SUBLANES, NQ)).astype(BF16)

        def body(t, acc):
            xb = rscore_ref[tile(t), :].reshape(ST // (2 * SUBLANES), 2 * SUBLANES, NQ)
            ind = jnp.where(xb < cb[None], jnp.zeros((), BF16), jnp.ones((), BF16))
            for chain in range(ACC_CHAINS):
                part = ind[chain]
                for r in range(chain + ACC_CHAINS, ind.shape[0], ACC_CHAINS):
                    part = part + ind[r]
                acc = acc + part.astype(F32)
            return acc
        acc = lax.fori_loop(0, nt, body, jnp.zeros((2 * SUBLANES, NQ), F32))
        return jnp.sum(acc, axis=0, keepdims=True)

    def bf16_key_to_key(u16):
        return lax.shift_left(u16, 16) | jnp.where(u16 < 2 ** 15, 2 ** 16 - 1, 0)

    def coarse_body(i, u16):
        trial = u16 | lax.shift_left(jnp.int32(1), 15 - i)
        c = count_rounded(_key_to_float(bf16_key_to_key(trial)))
        return jnp.where(c >= topk, trial, u16)

    u16 = lax.fori_loop(0, 16, coarse_body, jnp.zeros((1, NQ), jnp.int32))
    base = bf16_key_to_key(u16) - FINE_SPAN // 2

    def fine_step(i, delta, settled):
        trial = delta | lax.shift_left(jnp.int32(1), FINE_BITS - 1 - i)
        cand = _key_to_float(base + trial)
        c = count(lambda x, r0: jnp.where(x < cand, 0.0, 1.0))
        delta = jnp.where(settled > 0.0, delta, jnp.where(c >= topk, trial, delta))
        return delta, jnp.where(c == topk, 1.0, settled)

    def unsettled(settled):
        return jnp.min(settled) == 0.0

    def fine_body(carry):
        i, delta, settled, _ = carry
        delta, settled = fine_step(i, delta, settled)
        delta, settled = fine_step(i + 1, delta, settled)
        return i + 2, delta, settled, unsettled(settled)

    delta, settled = fine_step(0, jnp.zeros((1, NQ), jnp.int32), jnp.where(lane >= nq_valid, 1.0, 0.0))
    _, delta, settled, open_lanes = lax.while_loop(lambda carry: (carry[0] < FINE_BITS) & carry[3], fine_body,
                                                   (jnp.int32(1), delta, settled, unsettled(settled)))
    prefix = base + delta
    thr = jnp.maximum(_key_to_float(prefix), jnp.finfo(F32).min)

    def tied_need():
        return topk - count(lambda x, r0: jnp.where(x > thr, 1.0, 0.0))

    tie_ref[0] = jnp.int32(0)

    @pl.when(open_lanes)
    def _():
        n_eq = count(lambda x, r0: jnp.where(x == thr, 1.0, 0.0))
        tie = jnp.where(n_eq > tied_need(), 1.0 - settled, 0.0)
        tie_ref[0] = (jnp.max(tie) > 0.0).astype(jnp.int32)

    has_tie = tie_ref[0] > 0

    @pl.when(jnp.logical_not(has_tie))
    def _():
        def body(t, carry):
            score_ref[tile(t), :] = jnp.where(score_ref[tile(t), :] >= thr, 0.0, neg_inf)
            return carry
        lax.fori_loop(0, nt, body, 0)

    @pl.when(has_tie)
    def _():
        n_bits = int(s_pad).bit_length()
        need = tied_need()

        def idx_bit(i, cut):
            trial = cut | lax.shift_left(jnp.int32(1), n_bits - 1 - i)
            c = count(lambda x, r0: jnp.where(x == thr, jnp.where(r0 + row_iota < trial, 1.0, 0.0), 0.0))
            return jnp.where(c <= need, trial, cut)

        cut = lax.fori_loop(0, n_bits, idx_bit, jnp.zeros((1, NQ), jnp.int32))

        def body(t, carry):
            x = score_ref[tile(t), :]
            tied = jnp.where(t * ST + row_iota < cut, 0.0, neg_inf)
            score_ref[tile(t), :] = jnp.where(x > thr, 0.0, jnp.where(x == thr, tied, neg_inf))
            return carry
        lax.fori_loop(0, nt, body, 0)

    if causal:
        @pl.when(nt_att > nt)
        def _():
            score_ref[tile(nt), :] = jnp.full((ST, NQ), neg_inf, F32)

    acc_ref[...] = jnp.zeros(acc_ref.shape, F32)

    def logits(t, slot):
        bias = score_ref[tile(t), :]
        for p in range(N_HEADS // 2):
            lp = jnp.dot(k_ref[tile(t), p * 2 * HEAD_DIM:(p + 1) * 2 * HEAD_DIM], wq_ref[p],
                         preferred_element_type=F32)
            for hh in range(2):
                h = 2 * p + hh
                s = lp[:, hh * NQ:(hh + 1) * NQ] + bias
                s_ref[slot, h] = s
                mt_ref[slot, h:h + 1, :] = jnp.max(_fold(s, jnp.max), axis=0, keepdims=True)
        return mt_ref[slot]

    def rescale(m_old, mt):
        m_new = jnp.maximum(m_old, mt)
        m_safe = jnp.where(m_new == neg_inf, 0.0, m_new)
        return m_new, m_safe, jnp.exp2(m_old - m_safe)

    def accumulate(t, slot, m_safe, alpha):
        for h in range(N_HEADS):
            rows = slice(h * V_ROWS, (h + 1) * V_ROWS)
            pr = jnp.exp2(s_ref[slot, h] - m_safe[h:h + 1, :]).astype(BF16)
            pv = jnp.dot(vT_ref[rows, tile(t)], pr, preferred_element_type=F32)
            acc_ref[rows, :] = alpha[h:h + 1, :] * acc_ref[rows, :] + pv

    state = rescale(jnp.full((N_HEADS, NQ), neg_inf, F32), logits(0, 0))

    def att_body(u, state):
        m, m_safe, alpha = state
        mt = logits(2 * u + 1, 1)
        accumulate(2 * u, 0, m_safe, alpha)
        m, m_safe, alpha = rescale(m, mt)
        mt = logits(2 * u + 2, 0)
        accumulate(2 * u + 1, 1, m_safe, alpha)
        return rescale(m, mt)

    m, m_safe, alpha = lax.fori_loop(0, nt_att // 2 - 1, att_body, state)
    mt = logits(nt_att - 1, 1)
    accumulate(nt_att - 2, 0, m_safe, alpha)
    m, m_safe, alpha = rescale(m, mt)
    accumulate(nt_att - 1, 1, m_safe, alpha)

    for h in range(N_HEADS):
        den = acc_ref[h * V_ROWS + HEAD_DIM:h * V_ROWS + HEAD_DIM + 1, :]
        outT_ref[h * HEAD_DIM:(h + 1) * HEAD_DIM, :] = acc_ref[h * V_ROWS:h * V_ROWS + HEAD_DIM, :] * (1.0 / den)
    o_ref[...] = outT_ref[...].T


def _dsa(qT, iqT, iwT, ikb, kb, vT_aug, *, causal, n_keys, nq_valid, topk):
    b, _, t = qT.shape
    s_pad = kb.shape[1]
    kern = functools.partial(_dsa_kernel, causal=causal, n_keys=n_keys, s_pad=s_pad, nq_valid=nq_valid, topk=topk)
    per_q = lambda rows: pl.BlockSpec((None, rows, NQ), lambda bi, j: (bi, 0, j))
    per_b = lambda r, c: pl.BlockSpec((None, r, c), lambda bi, j: (bi, 0, 0), pipeline_mode=pl.Buffered(1))
    return pl.pallas_call(
        kern,
        grid=(b, t // NQ),
        in_specs=[per_q(ATTN_WIDTH), per_q(N_IDX_HEADS * IDX_DIM), per_q(N_IDX_HEADS),
                  per_b(s_pad, IDX_DIM), per_b(s_pad, ATTN_WIDTH), per_b(N_HEADS * V_ROWS, s_pad)],
        out_specs=pl.BlockSpec((None, NQ, ATTN_WIDTH), lambda bi, j: (bi, j, 0)),
        out_shape=jax.ShapeDtypeStruct((b, t, ATTN_WIDTH), F32),
        scratch_shapes=[pltpu.VMEM((s_pad, NQ), F32),
                        pltpu.VMEM((s_pad, NQ), BF16),
                        pltpu.VMEM((IDX_DIM, N_IDX_HEADS * NQ), BF16),
                        pltpu.VMEM((N_HEADS // 2, 2 * HEAD_DIM, 2 * NQ), BF16),
                        pltpu.VMEM((2, N_HEADS, ST, NQ), F32),
                        pltpu.VMEM((N_HEADS * V_ROWS, NQ), F32),
                        pltpu.VMEM((ATTN_WIDTH, NQ), F32),
                        pltpu.VMEM((2, N_HEADS, NQ), F32),
                        pltpu.SMEM((1,), jnp.int32)],
        compiler_params=_params(2),
        name="dsa",
    )(qT, iqT, iwT, ikb, kb, vT_aug)


def _mix_out_kernel(x_ref, a_ref, u_ref, vg_ref, ws_ref, bsT_ref, wo_ref, g_ref, y_ref, gated_ref, *, n):
    tm = x_ref.shape[0]
    r = lax.broadcasted_iota(jnp.int32, (n, n), 0)
    c = lax.broadcasted_iota(jnp.int32, (n, n), 1)
    for g in range(GMLP_GROUPS):
        cols = slice(g * GMLP_GROUP_DIM, (g + 1) * GMLP_GROUP_DIM)
        w = jnp.where(c <= r, ws_ref[g], 0.0).astype(BF16)
        bias = bsT_ref[:, g:g + 1]
        for ch in range(tm // n):
            rows = slice(ch * n, (ch + 1) * n)
            s = jnp.dot(w, vg_ref[rows, cols].astype(BF16), preferred_element_type=F32) + bias
            gated_ref[rows, cols] = u_ref[rows, cols] * s
    mix = jnp.dot(a_ref[...].astype(BF16), wo_ref[:ATTN_WIDTH, :], preferred_element_type=F32)
    mix = mix + jnp.dot(gated_ref[...].astype(BF16), wo_ref[ATTN_WIDTH:, :], preferred_element_type=F32)
    y_ref[...] = x_ref[...] + _rms(mix, g_ref[...])


def _mix_out(x2d, attn, u, vg, ws_n, bsT_n, wo_b, g_post, *, n, tm):
    ntok = x2d.shape[0]
    tok = lambda w: pl.BlockSpec((tm, w), lambda i: (i, 0))
    return pl.pallas_call(
        functools.partial(_mix_out_kernel, n=n),
        grid=(ntok // tm,),
        in_specs=[tok(D_MODEL), tok(ATTN_WIDTH), tok(GMLP_WIDTH), tok(GMLP_WIDTH),
                  _const_spec((GMLP_GROUPS, n, n)), _const_spec((n, GMLP_GROUPS)),
                  _const_spec((ATTN_WIDTH + GMLP_WIDTH, D_MODEL)), _const_spec((1, D_MODEL))],
        out_specs=tok(D_MODEL),
        out_shape=jax.ShapeDtypeStruct((ntok, D_MODEL), F32),
        scratch_shapes=[pltpu.VMEM((tm, GMLP_WIDTH), F32)],
        compiler_params=_params(1),
        name="mix_out",
    )(x2d, attn, u, vg, ws_n, bsT_n, wo_b, g_post)


def _mem_kv_kernel(m_ref, g_ref, wk_ref, wv_ref, mk_ref, mv_ref):
    h = _rms(m_ref[...], g_ref[...]).astype(BF16)
    mk_ref[...] = jnp.dot(h, wk_ref[...], preferred_element_type=F32)
    mv_ref[...] = jnp.dot(h, wv_ref[...], preferred_element_type=F32)


def _mem_kv(mem2d, g, wk_b, wv_b):
    n = mem2d.shape[0]
    tm = min(512, n)
    tok = lambda w: pl.BlockSpec((tm, w), lambda i: (i, 0))
    return pl.pallas_call(
        _mem_kv_kernel,
        grid=(n // tm,),
        in_specs=[tok(D_MODEL), _const_spec((1, D_MODEL)), _const_spec((D_MODEL, MEM_WIDTH)),
                  _const_spec((D_MODEL, MEM_WIDTH))],
        out_specs=[tok(MEM_WIDTH), tok(MEM_WIDTH)],
        out_shape=[jax.ShapeDtypeStruct((n, MEM_WIDTH), F32)] * 2,
        compiler_params=_params(1),
        name="mem_kv",
    )(mem2d, g, wk_b, wv_b)


def _mem_attend_kernel(x_ref, mkT_ref, mv_ref, gpre_ref, wq_ref, wo_ref, gpost_ref, y_ref, o_ref):
    x = x_ref[...]
    h = _rms(x, gpre_ref[...]).astype(BF16)
    q = jnp.dot(h, wq_ref[...], preferred_element_type=F32).astype(BF16)
    for hd in range(MEM_HEADS):
        cols = slice(hd * MEM_HEAD_DIM, (hd + 1) * MEM_HEAD_DIM)
        lg = jnp.dot(q[:, cols], mkT_ref[cols, :], preferred_element_type=F32) * (MEM_HEAD_DIM ** -0.5)
        e = jnp.exp(lg - jnp.max(lg, axis=-1, keepdims=True))
        p = e / jnp.sum(e, axis=-1, keepdims=True)
        o_ref[:, cols] = jnp.dot(p.astype(BF16), mv_ref[:, cols], preferred_element_type=F32)
    out = jnp.dot(o_ref[...].astype(BF16), wo_ref[...], preferred_element_type=F32)
    y_ref[...] = x + _rms(out, gpost_ref[...])


def _mem_attend(x2d, mkT, mvb, g_pre, wq_b, wo_b, g_post, *, tm, tiles_per_batch):
    ntok = x2d.shape[0]
    n_mem = mvb.shape[1]
    tok = lambda w: pl.BlockSpec((tm, w), lambda i: (i, 0))
    return pl.pallas_call(
        _mem_attend_kernel,
        grid=(ntok // tm,),
        in_specs=[tok(D_MODEL),
                  pl.BlockSpec((None, MEM_WIDTH, n_mem), lambda i: (i // tiles_per_batch, 0, 0)),
                  pl.BlockSpec((None, n_mem, MEM_WIDTH), lambda i: (i // tiles_per_batch, 0, 0)),
                  _const_spec((1, D_MODEL)), _const_spec((D_MODEL, MEM_WIDTH)),
                  _const_spec((MEM_WIDTH, D_MODEL)), _const_spec((1, D_MODEL))],
        out_specs=tok(D_MODEL),
        out_shape=jax.ShapeDtypeStruct((ntok, D_MODEL), F32),
        scratch_shapes=[pltpu.VMEM((tm, MEM_WIDTH), F32)],
        compiler_params=_params(1),
        name="mem_attend",
    )(x2d, mkT, mvb, g_pre, wq_b, wo_b, g_post)


FF_CHUNK = 1024


def _ffn_kernel(x_ref, gpre_ref, w1_ref, w2_ref, gpost_ref, y_ref):
    x = x_ref[...]
    h = _rms(x, gpre_ref[...]).astype(BF16)
    d_ff = w1_ref.shape[1]
    out = jnp.zeros(x.shape, F32)
    for c in range(d_ff // FF_CHUNK):
        cols = slice(c * FF_CHUNK, (c + 1) * FF_CHUNK)
        a = jnp.maximum(jnp.dot(h, w1_ref[:, cols], preferred_element_type=F32), 0.0)
        out = out + jnp.dot((a * a).astype(BF16), w2_ref[cols, :], preferred_element_type=F32)
    y_ref[...] = x + _rms(out, gpost_ref[...])


def _ffn(x2d, g_pre, w1_b, w2_b, g_post):
    ntok = x2d.shape[0]
    tm = min(512, ntok)
    d_ff = w1_b.shape[1]
    tok = pl.BlockSpec((tm, D_MODEL), lambda i: (i, 0))
    return pl.pallas_call(
        _ffn_kernel,
        grid=(ntok // tm,),
        in_specs=[tok, _const_spec((1, D_MODEL)), _const_spec((D_MODEL, d_ff)), _const_spec((d_ff, D_MODEL)),
                  _const_spec((1, D_MODEL))],
        out_specs=tok,
        out_shape=jax.ShapeDtypeStruct((ntok, D_MODEL), F32),
        compiler_params=_params(1),
        name="ffn",
    )(x2d, g_pre, w1_b, w2_b, g_post)


def _pad_axis(a, axis, size):
    pad = [(0, 0)] * a.ndim
    pad[axis] = (0, size - a.shape[axis])
    return jnp.pad(a, pad)


def _augment_vT(vb3):
    b, s, _ = vb3.shape
    vT = jnp.swapaxes(vb3, 1, 2).reshape(b, N_HEADS, HEAD_DIM, s)
    ones = jnp.ones((b, N_HEADS, 1, s), BF16)
    zeros = jnp.zeros((b, N_HEADS, V_ROWS - HEAD_DIM - 1, s), BF16)
    return jnp.concatenate([vT, ones, zeros], axis=2).reshape(b, N_HEADS * V_ROWS, s)


def _group(x, w, *, past=None, mem_kv=None):
    b, t, _ = x.shape
    x2d = x.reshape(b * t, D_MODEL)
    q, k, kb, v, vb, iq, ikw, u, vg = _proj(x2d, w["g_mix_pre"], w["w_pad"], w["ln_g"], w["ln_b"])
    ik = ikw[:, :IDX_DIM]
    per_stream = lambda a: a.reshape(b, t, a.shape[-1])
    kb3, vb3, ikb3 = per_stream(kb), per_stream(vb), per_stream(ik.astype(BF16))
    qT = jnp.swapaxes(per_stream(q), 1, 2)
    iqT = jnp.swapaxes(per_stream(iq), 1, 2)
    iwT = jnp.swapaxes(per_stream(ikw[:, _IW_LO:_IW_HI]), 1, 2)
    if past is None:
        n_keys, causal = t, True
    else:
        ck, cv, cik = past
        kb3 = jnp.concatenate([ck.astype(BF16), kb3], axis=1)
        vb3 = jnp.concatenate([cv.astype(BF16), vb3], axis=1)
        ikb3 = jnp.concatenate([cik.astype(BF16), ikb3], axis=1)
        n_keys, causal = kb3.shape[1], False
        tq = -(-t // NQ) * NQ
        qT, iqT, iwT = (_pad_axis(a, 2, tq) for a in (qT, iqT, iwT))
    s_pad = -(-n_keys // (2 * ST)) * (2 * ST)
    kb3, vb3, ikb3 = (_pad_axis(a, 1, s_pad) for a in (kb3, vb3, ikb3))
    topk = min(TOPK_MAX, n_keys // 4)
    attn = _dsa(qT, iqT, iwT, ikb3, kb3, _augment_vT(vb3),
                causal=causal, n_keys=n_keys, nq_valid=min(t, NQ), topk=topk)
    attn2d = attn[:, :t].reshape(b * t, ATTN_WIDTH)

    n = min(t, GMLP_CHUNK)
    tm = min(512, b * t)
    x2d = _mix_out(x2d, attn2d, u, vg, w["w_s"][:, :n, :n], jnp.swapaxes(w["b_s"][:, :n], 0, 1),
                   w["w_o"], w["g_mix_post"], n=n, tm=tm)
    mk, mv = mem_kv
    tm_mem = min(512, t)
    x2d = _mem_attend(x2d, jnp.swapaxes(mk, 1, 2).astype(BF16), mv.astype(BF16), w["g_mem_pre"], w["w_mq"],
                      w["w_mo"], w["g_mem_post"], tm=tm_mem, tiles_per_batch=t // tm_mem)
    x2d = _ffn(x2d, w["g_ffn_pre"], w["w1"], w["w2"], w["g_ffn_post"])
    new = (k.reshape(b, t, N_HEADS, HEAD_DIM), v.reshape(b, t, N_HEADS, HEAD_DIM), ik.reshape(b, t, IDX_DIM),
           vg.reshape(b, t, GMLP_GROUPS, GMLP_GROUP_DIM))
    return x2d.reshape(b, t, D_MODEL), new


def kernel(x_prompt, x_sample, cache_attn_k, cache_attn_v, cache_idx_k, cache_mem_k, cache_mem_v, mem_prompt, g_mix_pre, w_in, ln_gmlp_g, ln_gmlp_b, w_s, b_s, w_o, g_mix_post, g_mem_kv, w_mk, w_mv, g_mem_pre, w_mq, w_mo, g_mem_post, g_ffn_pre, w1, w2, g_ffn_post):
    depth = w_in.shape[0]
    xp, xs = x_prompt, x_sample
    bp, n_mem = mem_prompt.shape[0], mem_prompt.shape[1]
    bs = xs.shape[0]
    row = lambda a: a.reshape(1, -1)
    outs = [[] for _ in range(9)]
    for l in range(depth):
        split = _IKW0 + _IW_HI
        w_pad = jnp.concatenate([w_in[l][:, :split], jnp.zeros((D_MODEL, _ZU0 - split), F32), w_in[l][:, split:]],
                                axis=1).astype(BF16)
        w = dict(g_mix_pre=row(g_mix_pre[l]), w_pad=w_pad, ln_g=row(ln_gmlp_g[l]), ln_b=row(ln_gmlp_b[l]),
                 w_s=w_s[l], b_s=b_s[l], w_o=w_o[l].astype(BF16), g_mix_post=row(g_mix_post[l]),
                 g_mem_pre=row(g_mem_pre[l]), w_mq=w_mq[l].astype(BF16), w_mo=w_mo[l].astype(BF16),
                 g_mem_post=row(g_mem_post[l]), g_ffn_pre=row(g_ffn_pre[l]), w1=w1[l].astype(BF16),
                 w2=w2[l].astype(BF16), g_ffn_post=row(g_ffn_post[l]))
        mk, mv = _mem_kv(mem_prompt.reshape(bp * n_mem, D_MODEL), row(g_mem_kv[l]), w_mk[l].astype(BF16),
                         w_mv[l].astype(BF16))
        mk, mv = mk.reshape(bp, n_mem, MEM_WIDTH), mv.reshape(bp, n_mem, MEM_WIDTH)
        xp, (kp, vp, ikp, _) = _group(xp, w, mem_kv=(mk, mv))
        past = (cache_attn_k[l].reshape(bs, -1, ATTN_WIDTH), cache_attn_v[l].reshape(bs, -1, ATTN_WIDTH),
                cache_idx_k[l])
        cmk = cache_mem_k[l].reshape(bs, -1, MEM_WIDTH)
        cmv = cache_mem_v[l].reshape(bs, -1, MEM_WIDTH)
        xs, (ks, vs, iks, gvs) = _group(xs, w, past=past, mem_kv=(cmk, cmv))
        new = (kp, vp, ikp, mk.reshape(bp, n_mem, MEM_HEADS, MEM_HEAD_DIM),
               mv.reshape(bp, n_mem, MEM_HEADS, MEM_HEAD_DIM), ks, vs, iks, gvs)
        for lst, a in zip(outs, new):
            lst.append(a)
    return (xp, xs) + tuple(jnp.stack(lst) for lst in outs)
```

```python
import functools
import math

import jax
import jax.numpy as jnp
from jax import lax
from jax.experimental import pallas as pl
from jax.experimental.pallas import tpu as pltpu

F32 = jnp.float32
BF16 = jnp.bfloat16

D_MODEL = 1024
N_HEADS = 8
HEAD_DIM = 64
ATTN_WIDTH = N_HEADS * HEAD_DIM
N_IDX_HEADS = 8
IDX_DIM = 64
GMLP_GROUPS = 4
GMLP_GROUP_DIM = 128
GMLP_WIDTH = GMLP_GROUPS * GMLP_GROUP_DIM
GMLP_CHUNK = 128
CHUNK = 64
TOPK_MAX = 256
MEM_HEADS = 4
MEM_HEAD_DIM = 128
MEM_WIDTH = MEM_HEADS * MEM_HEAD_DIM
EPS = 1e-6
LN_EPS = 1e-5

LANES = 128
SUBLANES = 8
VMEM_LIMIT = 56 * 1024 * 1024

_Q0, _K0, _V0, _IQ0, _IKW0, _ZU0, _ZV0, _WPAD = 0, 512, 1024, 1536, 2048, 2176, 2688, 3200
_IW_LO, _IW_HI = IDX_DIM, IDX_DIM + N_IDX_HEADS

NQ = 128
KT = 256
ST = 512
V_ROWS = 80
ACC_CHAINS = 4
FINE_BITS = 17
FINE_SPAN = 2 ** FINE_BITS


def _rms(x, g):
    return x * lax.rsqrt(jnp.mean(x * x, axis=-1, keepdims=True) + EPS) * g


def _params(n_axes):
    return pltpu.CompilerParams(dimension_semantics=("arbitrary",) * n_axes, vmem_limit_bytes=VMEM_LIMIT)


def _const_spec(shape):
    return pl.BlockSpec(shape, lambda *_: (0,) * len(shape), pipeline_mode=pl.Buffered(1))


def _proj_kernel(x_ref, g_ref, w_ref, lng_ref, lnb_ref, *out_refs, transposed):
    k_ref, kb_ref, v_ref, ikw_ref, ikb_ref, u_ref, vg_ref = out_refs[:7]
    h = _rms(x_ref[...], g_ref[...]).astype(BF16)

    def seg(lo, hi):
        return jnp.dot(h, w_ref[:, lo:hi], preferred_element_type=F32)

    q = seg(_Q0, _K0) * (HEAD_DIM ** -0.5 * math.log2(math.e))
    k = seg(_K0, _V0)
    k_ref[...] = k
    kb_ref[...] = k.astype(BF16)
    v = seg(_V0, _IQ0)
    v_ref[...] = v
    iq = seg(_IQ0, _IKW0) * (IDX_DIM ** -0.5)
    ikw = seg(_IKW0, _ZU0)
    lane = lax.broadcasted_iota(jnp.int32, ikw.shape, 1)
    is_iw = (lane >= _IW_LO) & (lane < _IW_HI)
    ikw = jnp.where(is_iw, ikw * (N_IDX_HEADS ** -0.5), ikw)
    ikw_ref[...] = ikw
    ikb_ref[...] = ikw.astype(BF16)
    u_ref[...] = jax.nn.gelu(seg(_ZU0, _ZV0))
    gv = jax.nn.gelu(seg(_ZV0, _WPAD))
    mu = jnp.mean(gv, axis=-1, keepdims=True)
    var = jnp.mean(jnp.square(gv - mu), axis=-1, keepdims=True)
    vg_ref[...] = (gv - mu) * lax.rsqrt(var + LN_EPS) * lng_ref[...] + lnb_ref[...]

    if not transposed:
        q_ref, iq_ref, vb_ref = out_refs[7:]
        q_ref[...] = q.astype(BF16)
        iq_ref[...] = iq.astype(BF16)
        vb_ref[...] = v.astype(BF16)
        return
    qT_ref, iqT_ref, iwT_ref, vTa_ref = out_refs[7:]
    qT_ref[...] = q.T.astype(BF16)
    iqT_ref[...] = iq.T.astype(BF16)
    iwT_ref[...] = ikw.T[_IW_LO:_IW_HI, :]
    vT = v.T
    pad_row = lax.broadcasted_iota(jnp.int32, (V_ROWS - HEAD_DIM, vT.shape[1]), 0)
    ones_then_zeros = jnp.where(pad_row == 0, 1.0, 0.0).astype(BF16)
    for hd in range(N_HEADS):
        vTa_ref[hd * V_ROWS:hd * V_ROWS + HEAD_DIM, :] = vT[hd * HEAD_DIM:(hd + 1) * HEAD_DIM, :].astype(BF16)
        vTa_ref[hd * V_ROWS + HEAD_DIM:(hd + 1) * V_ROWS, :] = ones_then_zeros


def _proj(x2d, g_pre, w_pad, ln_g, ln_b, *, transposed):
    n = x2d.shape[0]
    tm = min(512, n)
    tok = lambda w: pl.BlockSpec((tm, w), lambda i: (i, 0))
    feat = lambda r: pl.BlockSpec((r, tm), lambda i: (0, i))
    outs = [(ATTN_WIDTH, F32), (ATTN_WIDTH, BF16), (ATTN_WIDTH, F32), (LANES, F32), (LANES, BF16),
            (GMLP_WIDTH, F32), (GMLP_WIDTH, F32)]
    out_specs = [tok(w) for w, _ in outs]
    out_shape = [jax.ShapeDtypeStruct((n, w), dt) for w, dt in outs]
    if transposed:
        extra = [(ATTN_WIDTH, BF16), (N_IDX_HEADS * IDX_DIM, BF16), (N_IDX_HEADS, F32), (N_HEADS * V_ROWS, BF16)]
        out_specs += [feat(r) for r, _ in extra]
        out_shape += [jax.ShapeDtypeStruct((r, n), dt) for r, dt in extra]
    else:
        extra = [(ATTN_WIDTH, BF16), (N_IDX_HEADS * IDX_DIM, BF16), (ATTN_WIDTH, BF16)]
        out_specs += [tok(w) for w, _ in extra]
        out_shape += [jax.ShapeDtypeStruct((n, w), dt) for w, dt in extra]
    return pl.pallas_call(
        functools.partial(_proj_kernel, transposed=transposed),
        grid=(n // tm,),
        in_specs=[tok(D_MODEL), _const_spec((1, D_MODEL)), _const_spec((D_MODEL, _WPAD)),
                  _const_spec((1, GMLP_WIDTH)), _const_spec((1, GMLP_WIDTH))],
        out_specs=out_specs,
        out_shape=out_shape,
        compiler_params=_params(1),
        name="proj",
    )(x2d, g_pre, w_pad, ln_g, ln_b)


def _key_to_float(u):
    bits = jnp.where(u < 0, u ^ jnp.int32(-2 ** 31), ~u)
    return lax.bitcast_convert_type(bits, F32)


def _fold(x, op):
    part = op(x.reshape(-1, ACC_CHAINS * SUBLANES, NQ), axis=0)
    return op(part.reshape(ACC_CHAINS, SUBLANES, NQ), axis=0)


def _dsa_kernel(qT_ref, iqT_ref, iwT_ref, ik_ref, k_ref, vT_ref, o_ref,
                score_ref, rscore_ref, widx_ref, wq_ref, s_ref, acc_ref, outT_ref, mt_ref,
                *, causal, n_keys, s_pad, nq_valid, topk):
    j = pl.program_id(1)
    lane = lax.broadcasted_iota(jnp.int32, (1, NQ), 1)
    if causal:
        pos = j * NQ + lane
        lim = (lax.shift_right_logical(pos, CHUNK.bit_length() - 1) + 1) * CHUNK
        nt = ((j + 1) * NQ + (ST - 1)) // ST
        nt_att = ((nt + 1) // 2) * 2
    else:
        lim = jnp.full((1, NQ), n_keys, jnp.int32)
        nt = nt_att = s_pad // ST
    neg_inf = jnp.float32(-jnp.inf)
    row_iota = lax.broadcasted_iota(jnp.int32, (ST, NQ), 0)

    def tile(t):
        return pl.ds(pl.multiple_of(t * ST, ST), ST)

    widx_ref[IDX_DIM:, :] = jnp.zeros((LANES - IDX_DIM, N_IDX_HEADS * NQ), BF16)
    for h in range(N_IDX_HEADS):
        widx_ref[:IDX_DIM, h * NQ:(h + 1) * NQ] = iqT_ref[h * IDX_DIM:(h + 1) * IDX_DIM, :]
    pair_row = lax.broadcasted_iota(jnp.int32, (2 * HEAD_DIM, NQ), 0)
    for p in range(N_HEADS // 2):
        blk = qT_ref[p * 2 * HEAD_DIM:(p + 1) * 2 * HEAD_DIM, :]
        zero = jnp.zeros_like(blk)
        wq_ref[p, :, 0:NQ] = jnp.where(pair_row < HEAD_DIM, blk, zero)
        wq_ref[p, :, NQ:2 * NQ] = jnp.where(pair_row >= HEAD_DIM, blk, zero)

    def idx_body(t, carry):
        for sub in range(ST // KT):
            r0 = pl.multiple_of(t * ST + sub * KT, KT)
            logits = jnp.dot(ik_ref[pl.ds(r0, KT), :], widx_ref[...], preferred_element_type=F32)
            sc = jnp.zeros((KT, NQ), F32)
            for h in range(N_IDX_HEADS):
                sc = sc + jnp.maximum(logits[:, h * NQ:(h + 1) * NQ], 0.0) * iwT_ref[h:h + 1, :]
            sc = jnp.where(r0 + row_iota[:KT] < lim, sc, neg_inf)
            score_ref[pl.ds(r0, KT), :] = sc
            rscore_ref[pl.ds(r0, KT), :] = sc.astype(BF16)
        return carry

    lax.fori_loop(0, nt, idx_body, 0)

    def count(indicator):
        def body(t, acc):
            return acc + _fold(indicator(score_ref[tile(t), :], t * ST), jnp.sum)
        acc = lax.fori_loop(0, nt, body, jnp.zeros((SUBLANES, NQ), F32))
        return jnp.sum(acc, axis=0, keepdims=True)

    def count_rounded(cand):
        cb = jnp.broadcast_to(cand, (2 * SUBLANES, NQ)).astype(BF16)

        def body(t, acc):
            xb = rscore_ref[tile(t), :].reshape(ST // (2 * SUBLANES), 2 * SUBLANES, NQ)
            ind = jnp.where(xb < cb[None], jnp.zeros((), BF16), jnp.ones((), BF16))
            for chain in range(ACC_CHAINS):
                part = ind[chain]
                for r in range(chain + ACC_CHAINS, ind.shape[0], ACC_CHAINS):
                    part = part + ind[r]
                acc = acc + part.astype(F32)
            return acc
        acc = lax.fori_loop(0, nt, body, jnp.zeros((2 * SUBLANES, NQ), F32))
        return jnp.sum(acc, axis=0, keepdims=True)

    def bf16_key_to_key(u16):
        return lax.shift_left(u16, 16) | jnp.where(u16 < 2 ** 15, 2 ** 16 - 1, 0)

    def coarse_body(i, u16):
        trial = u16 | lax.shift_left(jnp.int32(1), 15 - i)
        c = count_rounded(_key_to_float(bf16_key_to_key(trial)))
        return jnp.where(c >= topk, trial, u16)

    u16 = lax.fori_loop(0, 16, coarse_body, jnp.zeros((1, NQ), jnp.int32))
    base = bf16_key_to_key(u16) - FINE_SPAN // 2

    def fine_step(i, delta, settled):
        trial = delta | lax.shift_left(jnp.int32(1), FINE_BITS - 1 - i)
        cand = _key_to_float(base + trial)
        c = count(lambda x, r0: jnp.where(x < cand, 0.0, 1.0))
        delta = jnp.where(settled > 0.0, delta, jnp.where(c >= topk, trial, delta))
        return delta, jnp.where(c == topk, 1.0, settled)

    def unsettled(settled):
        return jnp.min(settled) == 0.0

    def fine_body(carry):
        i, delta, settled, _ = carry
        delta, settled = fine_step(i, delta, settled)
        delta, settled = fine_step(i + 1, delta, settled)
        return i + 2, delta, settled, unsettled(settled)

    delta, settled = fine_step(0, jnp.zeros((1, NQ), jnp.int32), jnp.where(lane >= nq_valid, 1.0, 0.0))
    _, delta, settled, open_lanes = lax.while_loop(lambda carry: (carry[0] < FINE_BITS) & carry[3], fine_body,
                                                   (jnp.int32(1), delta, settled, unsettled(settled)))
    prefix = base + delta
    thr = jnp.maximum(_key_to_float(prefix), jnp.finfo(F32).min)

    @pl.when(jnp.logical_not(open_lanes))
    def _():
        def body(t, carry):
            score_ref[tile(t), :] = jnp.where(score_ref[tile(t), :] >= thr, 0.0, neg_inf)
            return carry
        lax.fori_loop(0, nt, body, 0)

    @pl.when(open_lanes)
    def _():
        need = topk - count(lambda x, r0: jnp.where(x > thr, 1.0, 0.0))
        r = lax.broadcasted_iota(jnp.int32, (ST, ST), 0)
        c = lax.broadcasted_iota(jnp.int32, (ST, ST), 1)
        lower = jnp.where(c <= r, 1.0, 0.0).astype(BF16)

        def body(t, seen):
            x = score_ref[tile(t), :]
            tied = jnp.where(x == thr, 1.0, 0.0)
            rank = seen + jnp.dot(lower, tied.astype(BF16), preferred_element_type=F32)
            keep = jnp.where(rank <= need, 0.0, neg_inf)
            score_ref[tile(t), :] = jnp.where(x > thr, 0.0, jnp.where(x == thr, keep, neg_inf))
            return rank[ST - 1:ST, :]
        lax.fori_loop(0, nt, body, jnp.zeros((1, NQ), F32))

    if causal:
        @pl.when(nt_att > nt)
        def _():
            score_ref[tile(nt), :] = jnp.full((ST, NQ), neg_inf, F32)

    acc_ref[...] = jnp.zeros(acc_ref.shape, F32)

    def logits(t, slot):
        bias = score_ref[tile(t), :]
        for p in range(N_HEADS // 2):
            lp = jnp.dot(k_ref[tile(t), p * 2 * HEAD_DIM:(p + 1) * 2 * HEAD_DIM], wq_ref[p],
                         preferred_element_type=F32)
            for hh in range(2):
                h = 2 * p + hh
                s = lp[:, hh * NQ:(hh + 1) * NQ] + bias
                s_ref[slot, h] = s
                mt_ref[slot, h:h + 1, :] = jnp.max(_fold(s, jnp.max), axis=0, keepdims=True)
        return mt_ref[slot]

    def rescale(m_old, mt):
        m_new = jnp.maximum(m_old, mt)
        m_safe = jnp.where(m_new == neg_inf, 0.0, m_new)
        return m_new, m_safe, jnp.exp2(m_old - m_safe)

    def accumulate(t, slot, m_safe, alpha):
        for h in range(N_HEADS):
            rows = slice(h * V_ROWS, (h + 1) * V_ROWS)
            pr = jnp.exp2(s_ref[slot, h] - m_safe[h:h + 1, :]).astype(BF16)
            pv = jnp.dot(vT_ref[rows, tile(t)], pr, preferred_element_type=F32)
            acc_ref[rows, :] = alpha[h:h + 1, :] * acc_ref[rows, :] + pv

    state = rescale(jnp.full((N_HEADS, NQ), neg_inf, F32), logits(0, 0))

    def att_body(u, state):
        m, m_safe, alpha = state
        mt = logits(2 * u + 1, 1)
        accumulate(2 * u, 0, m_safe, alpha)
        m, m_safe, alpha = rescale(m, mt)
        mt = logits(2 * u + 2, 0)
        accumulate(2 * u + 1, 1, m_safe, alpha)
        return rescale(m, mt)

    m, m_safe, alpha = lax.fori_loop(0, nt_att // 2 - 1, att_body, state)
    mt = logits(nt_att - 1, 1)
    accumulate(nt_att - 2, 0, m_safe, alpha)
    m, m_safe, alpha = rescale(m, mt)
    accumulate(nt_att - 1, 1, m_safe, alpha)

    for h in range(N_HEADS):
        den = acc_ref[h * V_ROWS + HEAD_DIM:h * V_ROWS + HEAD_DIM + 1, :]
        outT_ref[h * HEAD_DIM:(h + 1) * HEAD_DIM, :] = acc_ref[h * V_ROWS:h * V_ROWS + HEAD_DIM, :] * (1.0 / den)
    o_ref[...] = outT_ref[...].T


def _dsa(qT, iqT, iwT, ikb, kb, vT_aug, *, n_streams, causal, n_keys, nq_valid, topk):
    s_pad = kb.shape[0] // n_streams
    n_qblk = qT.shape[1] // (n_streams * NQ)
    kern = functools.partial(_dsa_kernel, causal=causal, n_keys=n_keys, s_pad=s_pad, nq_valid=nq_valid, topk=topk)
    per_q = lambda rows: pl.BlockSpec((rows, NQ), lambda bi, j: (0, bi * n_qblk + j))
    one_buf = pl.Buffered(1)
    return pl.pallas_call(
        kern,
        grid=(n_streams, n_qblk),
        in_specs=[per_q(ATTN_WIDTH), per_q(N_IDX_HEADS * IDX_DIM), per_q(N_IDX_HEADS),
                  pl.BlockSpec((s_pad, LANES), lambda bi, j: (bi, 0), pipeline_mode=one_buf),
                  pl.BlockSpec((s_pad, ATTN_WIDTH), lambda bi, j: (bi, 0), pipeline_mode=one_buf),
                  pl.BlockSpec((N_HEADS * V_ROWS, s_pad), lambda bi, j: (0, bi), pipeline_mode=one_buf)],
        out_specs=pl.BlockSpec((NQ, ATTN_WIDTH), lambda bi, j: (bi * n_qblk + j, 0)),
        out_shape=jax.ShapeDtypeStruct((n_streams * n_qblk * NQ, ATTN_WIDTH), F32),
        scratch_shapes=[pltpu.VMEM((s_pad, NQ), F32),
                        pltpu.VMEM((s_pad, NQ), BF16),
                        pltpu.VMEM((LANES, N_IDX_HEADS * NQ), BF16),
                        pltpu.VMEM((N_HEADS // 2, 2 * HEAD_DIM, 2 * NQ), BF16),
                        pltpu.VMEM((2, N_HEADS, ST, NQ), F32),
                        pltpu.VMEM((N_HEADS * V_ROWS, NQ), F32),
                        pltpu.VMEM((ATTN_WIDTH, NQ), F32),
                        pltpu.VMEM((2, N_HEADS, NQ), F32)],
        compiler_params=_params(2),
        name="dsa",
    )(qT, iqT, iwT, ikb, kb, vT_aug)


def _mix_out_kernel(x_ref, a_ref, u_ref, vg_ref, ws_ref, bsT_ref, wo_ref, g_ref, y_ref, gated_ref, *, n):
    tm = x_ref.shape[0]
    r = lax.broadcasted_iota(jnp.int32, (n, n), 0)
    c = lax.broadcasted_iota(jnp.int32, (n, n), 1)
    for g in range(GMLP_GROUPS):
        cols = slice(g * GMLP_GROUP_DIM, (g + 1) * GMLP_GROUP_DIM)
        w = jnp.where(c <= r, ws_ref[g], 0.0).astype(BF16)
        bias = bsT_ref[:, g:g + 1]
        for ch in range(tm // n):
            rows = slice(ch * n, (ch + 1) * n)
            s = jnp.dot(w, vg_ref[rows, cols].astype(BF16), preferred_element_type=F32) + bias
            gated_ref[rows, cols] = u_ref[rows, cols] * s
    mix = jnp.dot(a_ref[...].astype(BF16), wo_ref[:ATTN_WIDTH, :], preferred_element_type=F32)
    mix = mix + jnp.dot(gated_ref[...].astype(BF16), wo_ref[ATTN_WIDTH:, :], preferred_element_type=F32)
    y_ref[...] = x_ref[...] + _rms(mix, g_ref[...])


def _mix_out(x2d, attn, u, vg, ws_n, bsT_n, wo_b, g_post, *, n, tm):
    ntok = x2d.shape[0]
    tok = lambda w: pl.BlockSpec((tm, w), lambda i: (i, 0))
    return pl.pallas_call(
        functools.partial(_mix_out_kernel, n=n),
        grid=(ntok // tm,),
        in_specs=[tok(D_MODEL), tok(ATTN_WIDTH), tok(GMLP_WIDTH), tok(GMLP_WIDTH),
                  _const_spec((GMLP_GROUPS, n, n)), _const_spec((n, GMLP_GROUPS)),
                  _const_spec((ATTN_WIDTH + GMLP_WIDTH, D_MODEL)), _const_spec((1, D_MODEL))],
        out_specs=tok(D_MODEL),
        out_shape=jax.ShapeDtypeStruct((ntok, D_MODEL), F32),
        scratch_shapes=[pltpu.VMEM((tm, GMLP_WIDTH), F32)],
        compiler_params=_params(1),
        name="mix_out",
    )(x2d, attn, u, vg, ws_n, bsT_n, wo_b, g_post)


def _mem_kv_kernel(m_ref, g_ref, wk_ref, wv_ref, mk_ref, mv_ref):
    h = _rms(m_ref[...], g_ref[...]).astype(BF16)
    mk_ref[...] = jnp.dot(h, wk_ref[...], preferred_element_type=F32)
    mv_ref[...] = jnp.dot(h, wv_ref[...], preferred_element_type=F32)


def _mem_kv(mem2d, g, wk_b, wv_b):
    n = mem2d.shape[0]
    tm = min(512, n)
    tok = lambda w: pl.BlockSpec((tm, w), lambda i: (i, 0))
    return pl.pallas_call(
        _mem_kv_kernel,
        grid=(n // tm,),
        in_specs=[tok(D_MODEL), _const_spec((1, D_MODEL)), _const_spec((D_MODEL, MEM_WIDTH)),
                  _const_spec((D_MODEL, MEM_WIDTH))],
        out_specs=[tok(MEM_WIDTH), tok(MEM_WIDTH)],
        out_shape=[jax.ShapeDtypeStruct((n, MEM_WIDTH), F32)] * 2,
        compiler_params=_params(1),
        name="mem_kv",
    )(mem2d, g, wk_b, wv_b)


def _mem_attend_kernel(x_ref, mkT_ref, mv_ref, gpre_ref, wq_ref, wo_ref, gpost_ref, y_ref, o_ref):
    x = x_ref[...]
    h = _rms(x, gpre_ref[...]).astype(BF16)
    q = jnp.dot(h, wq_ref[...], preferred_element_type=F32).astype(BF16)
    for hd in range(MEM_HEADS):
        cols = slice(hd * MEM_HEAD_DIM, (hd + 1) * MEM_HEAD_DIM)
        lg = jnp.dot(q[:, cols], mkT_ref[cols, :], preferred_element_type=F32) * (MEM_HEAD_DIM ** -0.5)
        e = jnp.exp(lg - jnp.max(lg, axis=-1, keepdims=True))
        p = e / jnp.sum(e, axis=-1, keepdims=True)
        o_ref[:, cols] = jnp.dot(p.astype(BF16), mv_ref[:, cols], preferred_element_type=F32)
    out = jnp.dot(o_ref[...].astype(BF16), wo_ref[...], preferred_element_type=F32)
    y_ref[...] = x + _rms(out, gpost_ref[...])


def _mem_attend(x2d, mkT, mvb, g_pre, wq_b, wo_b, g_post, *, tm, tiles_per_batch):
    ntok = x2d.shape[0]
    n_mem = mvb.shape[1]
    tok = lambda w: pl.BlockSpec((tm, w), lambda i: (i, 0))
    return pl.pallas_call(
        _mem_attend_kernel,
        grid=(ntok // tm,),
        in_specs=[tok(D_MODEL),
                  pl.BlockSpec((None, MEM_WIDTH, n_mem), lambda i: (i // tiles_per_batch, 0, 0)),
                  pl.BlockSpec((None, n_mem, MEM_WIDTH), lambda i: (i // tiles_per_batch, 0, 0)),
                  _const_spec((1, D_MODEL)), _const_spec((D_MODEL, MEM_WIDTH)),
                  _const_spec((MEM_WIDTH, D_MODEL)), _const_spec((1, D_MODEL))],
        out_specs=tok(D_MODEL),
        out_shape=jax.ShapeDtypeStruct((ntok, D_MODEL), F32),
        scratch_shapes=[pltpu.VMEM((tm, MEM_WIDTH), F32)],
        compiler_params=_params(1),
        name="mem_attend",
    )(x2d, mkT, mvb, g_pre, wq_b, wo_b, g_post)


FF_CHUNK = 1024


def _ffn_kernel(x_ref, gpre_ref, w1_ref, w2_ref, gpost_ref, y_ref):
    x = x_ref[...]
    h = _rms(x, gpre_ref[...]).astype(BF16)
    d_ff = w1_ref.shape[1]
    out = jnp.zeros(x.shape, F32)
    for c in range(d_ff // FF_CHUNK):
        cols = slice(c * FF_CHUNK, (c + 1) * FF_CHUNK)
        a = jnp.maximum(jnp.dot(h, w1_ref[:, cols], preferred_element_type=F32), 0.0)
        out = out + jnp.dot((a * a).astype(BF16), w2_ref[cols, :], preferred_element_type=F32)
    y_ref[...] = x + _rms(out, gpost_ref[...])


def _ffn(x2d, g_pre, w1_b, w2_b, g_post):
    ntok = x2d.shape[0]
    tm = min(512, ntok)
    d_ff = w1_b.shape[1]
    tok = pl.BlockSpec((tm, D_MODEL), lambda i: (i, 0))
    return pl.pallas_call(
        _ffn_kernel,
        grid=(ntok // tm,),
        in_specs=[tok, _const_spec((1, D_MODEL)), _const_spec((D_MODEL, d_ff)), _const_spec((d_ff, D_MODEL)),
                  _const_spec((1, D_MODEL))],
        out_specs=tok,
        out_shape=jax.ShapeDtypeStruct((ntok, D_MODEL), F32),
        compiler_params=_params(1),
        name="ffn",
    )(x2d, g_pre, w1_b, w2_b, g_post)


def _pad_axis(a, axis, size):
    pad = [(0, 0)] * a.ndim
    pad[axis] = (0, size - a.shape[axis])
    return jnp.pad(a, pad)


def _augment_vT(vb3):
    b, s, _ = vb3.shape
    vT = jnp.transpose(vb3.reshape(b, s, N_HEADS, HEAD_DIM), (2, 3, 0, 1))
    ones = jnp.ones((N_HEADS, 1, b, s), BF16)
    zeros = jnp.zeros((N_HEADS, V_ROWS - HEAD_DIM - 1, b, s), BF16)
    return jnp.concatenate([vT, ones, zeros], axis=1).reshape(N_HEADS * V_ROWS, b * s)


def _group(x, w, *, past=None, mem_kv=None):
    b, t, _ = x.shape
    x2d = x.reshape(b * t, D_MODEL)
    causal = past is None
    k, kb, v, ikw, ikb, u, vg, *dsa_in = _proj(x2d, w["g_mix_pre"], w["w_pad"], w["ln_g"], w["ln_b"],
                                              transposed=causal)
    ik = ikw[:, :IDX_DIM]
    if causal:
        qT, iqT, iwT, vT_aug = dsa_in
        n_keys = t
    else:
        q, iq, vb = dsa_in
        ck, cv, cik = past
        n_keys = ck.shape[1] + t
        s_pad = -(-n_keys // (2 * ST)) * (2 * ST)
        tq = -(-t // NQ) * NQ
        per_stream = lambda a: a.reshape(b, t, a.shape[-1])
        keys = lambda old, new: _pad_axis(jnp.concatenate([old.astype(BF16), per_stream(new)], axis=1), 1, s_pad)
        kb = keys(ck, kb).reshape(b * s_pad, ATTN_WIDTH)
        ikb = keys(_pad_axis(cik, 2, LANES), ikb).reshape(b * s_pad, LANES)
        vT_aug = _augment_vT(keys(cv, vb))
        queries = lambda a: jnp.transpose(_pad_axis(per_stream(a), 1, tq), (2, 0, 1)).reshape(a.shape[-1], b * tq)
        qT, iqT, iwT = queries(q), queries(iq), queries(ikw[:, _IW_LO:_IW_HI])
    topk = min(TOPK_MAX, n_keys // 4)
    attn = _dsa(qT, iqT, iwT, ikb, kb, vT_aug, n_streams=b, causal=causal, n_keys=n_keys,
                nq_valid=min(t, NQ), topk=topk)
    attn2d = attn.reshape(b, -1, ATTN_WIDTH)[:, :t].reshape(b * t, ATTN_WIDTH)

    n = min(t, GMLP_CHUNK)
    tm = min(512, b * t)
    x2d = _mix_out(x2d, attn2d, u, vg, w["w_s"][:, :n, :n], jnp.swapaxes(w["b_s"][:, :n], 0, 1),
                   w["w_o"], w["g_mix_post"], n=n, tm=tm)
    mk, mv = mem_kv
    tm_mem = min(512, t)
    x2d = _mem_attend(x2d, jnp.swapaxes(mk, 1, 2).astype(BF16), mv.astype(BF16), w["g_mem_pre"], w["w_mq"],
                      w["w_mo"], w["g_mem_post"], tm=tm_mem, tiles_per_batch=t // tm_mem)
    x2d = _ffn(x2d, w["g_ffn_pre"], w["w1"], w["w2"], w["g_ffn_post"])
    new = (k.reshape(b, t, N_HEADS, HEAD_DIM), v.reshape(b, t, N_HEADS, HEAD_DIM), ik.reshape(b, t, IDX_DIM),
           vg.reshape(b, t, GMLP_GROUPS, GMLP_GROUP_DIM))
    return x2d.reshape(b, t, D_MODEL), new


def kernel(x_prompt, x_sample, cache_attn_k, cache_attn_v, cache_idx_k, cache_mem_k, cache_mem_v, mem_prompt, g_mix_pre, w_in, ln_gmlp_g, ln_gmlp_b, w_s, b_s, w_o, g_mix_post, g_mem_kv, w_mk, w_mv, g_mem_pre, w_mq, w_mo, g_mem_post, g_ffn_pre, w1, w2, g_ffn_post):
    depth = w_in.shape[0]
    xp, xs = x_prompt, x_sample
    bp, n_mem = mem_prompt.shape[0], mem_prompt.shape[1]
    bs = xs.shape[0]
    row = lambda a: a.reshape(1, -1)
    outs = [[] for _ in range(9)]
    for l in range(depth):
        split = _IKW0 + _IW_HI
        w_pad = jnp.concatenate([w_in[l][:, :split], jnp.zeros((D_MODEL, _ZU0 - split), F32), w_in[l][:, split:]],
                                axis=1).astype(BF16)
        w = dict(g_mix_pre=row(g_mix_pre[l]), w_pad=w_pad, ln_g=row(ln_gmlp_g[l]), ln_b=row(ln_gmlp_b[l]),
                 w_s=w_s[l], b_s=b_s[l], w_o=w_o[l].astype(BF16), g_mix_post=row(g_mix_post[l]),
                 g_mem_pre=row(g_mem_pre[l]), w_mq=w_mq[l].astype(BF16), w_mo=w_mo[l].astype(BF16),
                 g_mem_post=row(g_mem_post[l]), g_ffn_pre=row(g_ffn_pre[l]), w1=w1[l].astype(BF16),
                 w2=w2[l].astype(BF16), g_ffn_post=row(g_ffn_post[l]))
        mk, mv = _mem_kv(mem_prompt.reshape(bp * n_mem, D_MODEL), row(g_mem_kv[l]), w_mk[l].astype(BF16),
                         w_mv[l].astype(BF16))
        mk, mv = mk.reshape(bp, n_mem, MEM_WIDTH), mv.reshape(bp, n_mem, MEM_WIDTH)
        xp, (kp, vp, ikp, _) = _group(xp, w, mem_kv=(mk, mv))
        past = (cache_attn_k[l].reshape(bs, -1, ATTN_WIDTH), cache_attn_v[l].reshape(bs, -1, ATTN_WIDTH),
                cache_idx_k[l])
        cmk = cache_mem_k[l].reshape(bs, -1, MEM_WIDTH)
        cmv = cache_mem_v[l].reshape(bs, -1, MEM_WIDTH)
        xs, (ks, vs, iks, gvs) = _group(xs, w, past=past, mem_kv=(cmk, cmv))
        new = (kp, vp, ikp, mk.reshape(bp, n_mem, MEM_HEADS, MEM_HEAD_DIM),
               mv.reshape(bp, n_mem, MEM_HEADS, MEM_HEAD_DIM), ks, vs, iks, gvs)
        for lst, a in zip(outs, new):
            lst.append(a)
    return (xp, xs) + tuple(jnp.stack(lst) for lst in outs)
```

```python
import functools
import math

import jax
import jax.numpy as jnp
from jax import lax
from jax.experimental import pallas as pl
from jax.experimental.pallas import tpu as pltpu

F32 = jnp.float32
BF16 = jnp.bfloat16

D_MODEL = 1024
N_HEADS = 8
HEAD_DIM = 64
ATTN_WIDTH = N_HEADS * HEAD_DIM
N_IDX_HEADS = 8
IDX_DIM = 64
GMLP_GROUPS = 4
GMLP_GROUP_DIM = 128
GMLP_WIDTH = GMLP_GROUPS * GMLP_GROUP_DIM
GMLP_CHUNK = 128
CHUNK = 64
TOPK_MAX = 256
MEM_HEADS = 4
MEM_HEAD_DIM = 128
MEM_WIDTH = MEM_HEADS * MEM_HEAD_DIM
EPS = 1e-6
LN_EPS = 1e-5

LANES = 128
SUBLANES = 8
VMEM_LIMIT = 56 * 1024 * 1024

_Q0, _K0, _V0, _IQ0, _IKW0, _ZU0, _ZV0, _WPAD = 0, 512, 1024, 1536, 2048, 2176, 2688, 3200
_IW_LO, _IW_HI = IDX_DIM, IDX_DIM + N_IDX_HEADS

NQ_PROMPT = 256
NQ_CHUNK = 128
KT = 256
ST = 512
V_ROWS = 80
ACC_CHAINS = 4
FINE_BITS = 17
FINE_SPAN = 2 ** FINE_BITS


def _rms(x, g):
    return x * lax.rsqrt(jnp.mean(x * x, axis=-1, keepdims=True) + EPS) * g


def _params(n_axes):
    return pltpu.CompilerParams(dimension_semantics=("arbitrary",) * n_axes, vmem_limit_bytes=VMEM_LIMIT)


def _const_spec(shape):
    return pl.BlockSpec(shape, lambda *_: (0,) * len(shape), pipeline_mode=pl.Buffered(1))


def _proj_kernel(x_ref, g_ref, w_ref, lng_ref, lnb_ref, *out_refs, transposed):
    k_ref, kb_ref, v_ref, ikw_ref, ikb_ref, u_ref, vg_ref = out_refs[:7]
    h = _rms(x_ref[...], g_ref[...]).astype(BF16)

    def seg(lo, hi):
        return jnp.dot(h, w_ref[:, lo:hi], preferred_element_type=F32)

    q = seg(_Q0, _K0) * (HEAD_DIM ** -0.5 * math.log2(math.e))
    k = seg(_K0, _V0)
    k_ref[...] = k
    kb_ref[...] = k.astype(BF16)
    v = seg(_V0, _IQ0)
    v_ref[...] = v
    iq = seg(_IQ0, _IKW0) * (IDX_DIM ** -0.5)
    ikw = seg(_IKW0, _ZU0)
    lane = lax.broadcasted_iota(jnp.int32, ikw.shape, 1)
    is_iw = (lane >= _IW_LO) & (lane < _IW_HI)
    ikw = jnp.where(is_iw, ikw * (N_IDX_HEADS ** -0.5), ikw)
    ikw_ref[...] = ikw
    ikb_ref[...] = ikw.astype(BF16)
    u_ref[...] = jax.nn.gelu(seg(_ZU0, _ZV0))
    gv = jax.nn.gelu(seg(_ZV0, _WPAD))
    mu = jnp.mean(gv, axis=-1, keepdims=True)
    var = jnp.mean(jnp.square(gv - mu), axis=-1, keepdims=True)
    vg_ref[...] = (gv - mu) * lax.rsqrt(var + LN_EPS) * lng_ref[...] + lnb_ref[...]

    if not transposed:
        q_ref, iq_ref, vb_ref = out_refs[7:]
        q_ref[...] = q.astype(BF16)
        iq_ref[...] = iq.astype(BF16)
        vb_ref[...] = v.astype(BF16)
        return
    qT_ref, iqT_ref, iwT_ref, vTa_ref = out_refs[7:]
    qT_ref[...] = q.T.astype(BF16)
    iqT_ref[...] = iq.T.astype(BF16)
    iwT_ref[...] = ikw.T[_IW_LO:_IW_HI, :]
    vT = v.T
    pad_row = lax.broadcasted_iota(jnp.int32, (V_ROWS - HEAD_DIM, vT.shape[1]), 0)
    ones_then_zeros = jnp.where(pad_row == 0, 1.0, 0.0).astype(BF16)
    for hd in range(N_HEADS):
        vTa_ref[hd * V_ROWS:hd * V_ROWS + HEAD_DIM, :] = vT[hd * HEAD_DIM:(hd + 1) * HEAD_DIM, :].astype(BF16)
        vTa_ref[hd * V_ROWS + HEAD_DIM:(hd + 1) * V_ROWS, :] = ones_then_zeros


def _proj(x2d, g_pre, w_pad, ln_g, ln_b, *, transposed):
    n = x2d.shape[0]
    tm = min(512, n)
    tok = lambda w: pl.BlockSpec((tm, w), lambda i: (i, 0))
    feat = lambda r: pl.BlockSpec((r, tm), lambda i: (0, i))
    outs = [(ATTN_WIDTH, F32), (ATTN_WIDTH, BF16), (ATTN_WIDTH, F32), (LANES, F32), (LANES, BF16),
            (GMLP_WIDTH, F32), (GMLP_WIDTH, F32)]
    out_specs = [tok(w) for w, _ in outs]
    out_shape = [jax.ShapeDtypeStruct((n, w), dt) for w, dt in outs]
    if transposed:
        extra = [(ATTN_WIDTH, BF16), (N_IDX_HEADS * IDX_DIM, BF16), (N_IDX_HEADS, F32), (N_HEADS * V_ROWS, BF16)]
        out_specs += [feat(r) for r, _ in extra]
        out_shape += [jax.ShapeDtypeStruct((r, n), dt) for r, dt in extra]
    else:
        extra = [(ATTN_WIDTH, BF16), (N_IDX_HEADS * IDX_DIM, BF16), (ATTN_WIDTH, BF16)]
        out_specs += [tok(w) for w, _ in extra]
        out_shape += [jax.ShapeDtypeStruct((n, w), dt) for w, dt in extra]
    return pl.pallas_call(
        functools.partial(_proj_kernel, transposed=transposed),
        grid=(n // tm,),
        in_specs=[tok(D_MODEL), _const_spec((1, D_MODEL)), _const_spec((D_MODEL, _WPAD)),
                  _const_spec((1, GMLP_WIDTH)), _const_spec((1, GMLP_WIDTH))],
        out_specs=out_specs,
        out_shape=out_shape,
        compiler_params=_params(1),
        name="proj",
    )(x2d, g_pre, w_pad, ln_g, ln_b)


def _key_to_float(u):
    bits = jnp.where(u < 0, u ^ jnp.int32(-2 ** 31), ~u)
    return lax.bitcast_convert_type(bits, F32)


def _fold(x, op):
    nq = x.shape[-1]
    part = op(x.reshape(-1, ACC_CHAINS * SUBLANES, nq), axis=0)
    return op(part.reshape(ACC_CHAINS, SUBLANES, nq), axis=0)


def _dsa_kernel(qT_ref, iqT_ref, iwT_ref, ik_ref, k_ref, vT_ref, o_ref,
                score_ref, rscore_ref, widx_ref, wq_ref, s_ref, acc_ref, outT_ref, mt_ref,
                *, nq, causal, n_keys, s_pad, nq_valid, topk):
    j = pl.program_id(1)
    lane = lax.broadcasted_iota(jnp.int32, (1, nq), 1)
    if causal:
        pos = j * nq + lane
        lim = (lax.shift_right_logical(pos, CHUNK.bit_length() - 1) + 1) * CHUNK
        nt = ((j + 1) * nq + (ST - 1)) // ST
        nt_att = ((nt + 1) // 2) * 2
    else:
        lim = jnp.full((1, nq), n_keys, jnp.int32)
        nt = nt_att = s_pad // ST
    neg_inf = jnp.float32(-jnp.inf)
    key_row = lax.broadcasted_iota(jnp.int32, (KT, nq), 0)

    def tile(t):
        return pl.ds(pl.multiple_of(t * ST, ST), ST)

    widx_ref[IDX_DIM:, :] = jnp.zeros((LANES - IDX_DIM, N_IDX_HEADS * nq), BF16)
    for h in range(N_IDX_HEADS):
        widx_ref[:IDX_DIM, h * nq:(h + 1) * nq] = iqT_ref[h * IDX_DIM:(h + 1) * IDX_DIM, :]
    pair_row = lax.broadcasted_iota(jnp.int32, (2 * HEAD_DIM, nq), 0)
    for p in range(N_HEADS // 2):
        blk = qT_ref[p * 2 * HEAD_DIM:(p + 1) * 2 * HEAD_DIM, :]
        zero = jnp.zeros_like(blk)
        wq_ref[p, :, 0:nq] = jnp.where(pair_row < HEAD_DIM, blk, zero)
        wq_ref[p, :, nq:2 * nq] = jnp.where(pair_row >= HEAD_DIM, blk, zero)

    def idx_body(t, carry):
        for sub in range(ST // KT):
            r0 = pl.multiple_of(t * ST + sub * KT, KT)
            logits = jnp.dot(ik_ref[pl.ds(r0, KT), :], widx_ref[...], preferred_element_type=F32)
            sc = jnp.zeros((KT, nq), F32)
            for h in range(N_IDX_HEADS):
                sc = sc + jnp.maximum(logits[:, h * nq:(h + 1) * nq], 0.0) * iwT_ref[h:h + 1, :]
            sc = jnp.where(r0 + key_row < lim, sc, neg_inf)
            score_ref[pl.ds(r0, KT), :] = sc
            rscore_ref[pl.ds(r0, KT), :] = sc.astype(BF16)
        return carry

    lax.fori_loop(0, nt, idx_body, 0)

    def count(indicator):
        def body(t, acc):
            return acc + _fold(indicator(score_ref[tile(t), :], t * ST), jnp.sum)
        acc = lax.fori_loop(0, nt, body, jnp.zeros((SUBLANES, nq), F32))
        return jnp.sum(acc, axis=0, keepdims=True)

    def count_rounded(cand):
        cb = jnp.broadcast_to(cand, (2 * SUBLANES, nq)).astype(BF16)

        def body(t, acc):
            xb = rscore_ref[tile(t), :].reshape(ST // (2 * SUBLANES), 2 * SUBLANES, nq)
            ind = jnp.where(xb < cb[None], jnp.zeros((), BF16), jnp.ones((), BF16))
            for chain in range(ACC_CHAINS):
                part = ind[chain]
                for r in range(chain + ACC_CHAINS, ind.shape[0], ACC_CHAINS):
                    part = part + ind[r]
                acc = acc + part.astype(F32)
            return acc
        acc = lax.fori_loop(0, nt, body, jnp.zeros((2 * SUBLANES, nq), F32))
        return jnp.sum(acc, axis=0, keepdims=True)

    def bf16_key_to_key(u16):
        return lax.shift_left(u16, 16) | jnp.where(u16 < 2 ** 15, 2 ** 16 - 1, 0)

    def coarse_body(i, u16):
        trial = u16 | lax.shift_left(jnp.int32(1), 15 - i)
        c = count_rounded(_key_to_float(bf16_key_to_key(trial)))
        return jnp.where(c >= topk, trial, u16)

    u16 = lax.fori_loop(0, 16, coarse_body, jnp.zeros((1, nq), jnp.int32))
    base = bf16_key_to_key(u16) - FINE_SPAN // 2

    def fine_step(i, delta, settled):
        trial = delta | lax.shift_left(jnp.int32(1), FINE_BITS - 1 - i)
        cand = _key_to_float(base + trial)
        c = count(lambda x, r0: jnp.where(x < cand, 0.0, 1.0))
        delta = jnp.where(settled > 0.0, delta, jnp.where(c >= topk, trial, delta))
        return delta, jnp.where(c == topk, 1.0, settled)

    def unsettled(settled):
        return jnp.min(settled) == 0.0

    def fine_body(carry):
        i, delta, settled, _ = carry
        delta, settled = fine_step(i, delta, settled)
        delta, settled = fine_step(i + 1, delta, settled)
        return i + 2, delta, settled, unsettled(settled)

    delta, settled = fine_step(0, jnp.zeros((1, nq), jnp.int32), jnp.where(lane >= nq_valid, 1.0, 0.0))
    _, delta, settled, open_lanes = lax.while_loop(lambda carry: (carry[0] < FINE_BITS) & carry[3], fine_body,
                                                   (jnp.int32(1), delta, settled, unsettled(settled)))
    prefix = base + delta
    thr = jnp.maximum(_key_to_float(prefix), jnp.finfo(F32).min)

    @pl.when(jnp.logical_not(open_lanes))
    def _():
        def body(t, carry):
            score_ref[tile(t), :] = jnp.where(score_ref[tile(t), :] >= thr, 0.0, neg_inf)
            return carry
        lax.fori_loop(0, nt, body, 0)

    @pl.when(open_lanes)
    def _():
        need = topk - count(lambda x, r0: jnp.where(x > thr, 1.0, 0.0))
        r = lax.broadcasted_iota(jnp.int32, (ST, ST), 0)
        c = lax.broadcasted_iota(jnp.int32, (ST, ST), 1)
        lower = jnp.where(c <= r, 1.0, 0.0).astype(BF16)

        def body(t, seen):
            x = score_ref[tile(t), :]
            tied = jnp.where(x == thr, 1.0, 0.0)
            rank = seen + jnp.dot(lower, tied.astype(BF16), preferred_element_type=F32)
            keep = jnp.where(rank <= need, 0.0, neg_inf)
            score_ref[tile(t), :] = jnp.where(x > thr, 0.0, jnp.where(x == thr, keep, neg_inf))
            return rank[ST - 1:ST, :]
        lax.fori_loop(0, nt, body, jnp.zeros((1, nq), F32))

    if causal:
        @pl.when(nt_att > nt)
        def _():
            score_ref[tile(nt), :] = jnp.full((ST, nq), neg_inf, F32)

    acc_ref[...] = jnp.zeros(acc_ref.shape, F32)

    def logits(t, slot):
        bias = score_ref[tile(t), :]
        for p in range(N_HEADS // 2):
            lp = jnp.dot(k_ref[tile(t), p * 2 * HEAD_DIM:(p + 1) * 2 * HEAD_DIM], wq_ref[p],
                         preferred_element_type=F32)
            for hh in range(2):
                h = 2 * p + hh
                s = lp[:, hh * nq:(hh + 1) * nq] + bias
                s_ref[slot, h] = s
                mt_ref[slot, h:h + 1, :] = jnp.max(_fold(s, jnp.max), axis=0, keepdims=True)
        return mt_ref[slot]

    def rescale(m_old, mt):
        m_new = jnp.maximum(m_old, mt)
        m_safe = jnp.where(m_new == neg_inf, 0.0, m_new)
        return m_new, m_safe, jnp.exp2(m_old - m_safe)

    def accumulate(t, slot, m_safe, alpha):
        for h in range(N_HEADS):
            rows = slice(h * V_ROWS, (h + 1) * V_ROWS)
            pr = jnp.exp2(s_ref[slot, h] - m_safe[h:h + 1, :]).astype(BF16)
            pv = jnp.dot(vT_ref[rows, tile(t)], pr, preferred_element_type=F32)
            acc_ref[rows, :] = alpha[h:h + 1, :] * acc_ref[rows, :] + pv

    state = rescale(jnp.full((N_HEADS, nq), neg_inf, F32), logits(0, 0))

    def att_body(u, state):
        m, m_safe, alpha = state
        mt = logits(2 * u + 1, 1)
        accumulate(2 * u, 0, m_safe, alpha)
        m, m_safe, alpha = rescale(m, mt)
        mt = logits(2 * u + 2, 0)
        accumulate(2 * u + 1, 1, m_safe, alpha)
        return rescale(m, mt)

    m, m_safe, alpha = lax.fori_loop(0, nt_att // 2 - 1, att_body, state)
    mt = logits(nt_att - 1, 1)
    accumulate(nt_att - 2, 0, m_safe, alpha)
    m, m_safe, alpha = rescale(m, mt)
    accumulate(nt_att - 1, 1, m_safe, alpha)

    for h in range(N_HEADS):
        den = acc_ref[h * V_ROWS + HEAD_DIM:h * V_ROWS + HEAD_DIM + 1, :]
        outT_ref[h * HEAD_DIM:(h + 1) * HEAD_DIM, :] = acc_ref[h * V_ROWS:h * V_ROWS + HEAD_DIM, :] * (1.0 / den)
    o_ref[...] = outT_ref[...].T


def _dsa(qT, iqT, iwT, ikb, kb, vT_aug, *, n_streams, nq, causal, n_keys, nq_valid, topk):
    s_pad = kb.shape[0] // n_streams
    n_qblk = qT.shape[1] // (n_streams * nq)
    kern = functools.partial(_dsa_kernel, nq=nq, causal=causal, n_keys=n_keys, s_pad=s_pad, nq_valid=nq_valid,
                             topk=topk)
    per_q = lambda rows: pl.BlockSpec((rows, nq), lambda bi, j: (0, bi * n_qblk + j))
    one_buf = pl.Buffered(1)
    return pl.pallas_call(
        kern,
        grid=(n_streams, n_qblk),
        in_specs=[per_q(ATTN_WIDTH), per_q(N_IDX_HEADS * IDX_DIM), per_q(N_IDX_HEADS),
                  pl.BlockSpec((s_pad, LANES), lambda bi, j: (bi, 0), pipeline_mode=one_buf),
                  pl.BlockSpec((s_pad, ATTN_WIDTH), lambda bi, j: (bi, 0), pipeline_mode=one_buf),
                  pl.BlockSpec((N_HEADS * V_ROWS, s_pad), lambda bi, j: (0, bi), pipeline_mode=one_buf)],
        out_specs=pl.BlockSpec((nq, ATTN_WIDTH), lambda bi, j: (bi * n_qblk + j, 0)),
        out_shape=jax.ShapeDtypeStruct((n_streams * n_qblk * nq, ATTN_WIDTH), F32),
        scratch_shapes=[pltpu.VMEM((s_pad, nq), F32),
                        pltpu.VMEM((s_pad, nq), BF16),
                        pltpu.VMEM((LANES, N_IDX_HEADS * nq), BF16),
                        pltpu.VMEM((N_HEADS // 2, 2 * HEAD_DIM, 2 * nq), BF16),
                        pltpu.VMEM((2, N_HEADS, ST, nq), F32),
                        pltpu.VMEM((N_HEADS * V_ROWS, nq), F32),
                        pltpu.VMEM((ATTN_WIDTH, nq), F32),
                        pltpu.VMEM((2, N_HEADS, nq), F32)],
        compiler_params=_params(2),
        name="dsa",
    )(qT, iqT, iwT, ikb, kb, vT_aug)


def _mix_out_kernel(x_ref, a_ref, u_ref, vg_ref, ws_ref, bsT_ref, wo_ref, g_ref, y_ref, gated_ref, *, n):
    tm = x_ref.shape[0]
    r = lax.broadcasted_iota(jnp.int32, (n, n), 0)
    c = lax.broadcasted_iota(jnp.int32, (n, n), 1)
    for g in range(GMLP_GROUPS):
        cols = slice(g * GMLP_GROUP_DIM, (g + 1) * GMLP_GROUP_DIM)
        w = jnp.where(c <= r, ws_ref[g], 0.0).astype(BF16)
        bias = bsT_ref[:, g:g + 1]
        for ch in range(tm // n):
            rows = slice(ch * n, (ch + 1) * n)
            s = jnp.dot(w, vg_ref[rows, cols].astype(BF16), preferred_element_type=F32) + bias
            gated_ref[rows, cols] = u_ref[rows, cols] * s
    mix = jnp.dot(a_ref[...].astype(BF16), wo_ref[:ATTN_WIDTH, :], preferred_element_type=F32)
    mix = mix + jnp.dot(gated_ref[...].astype(BF16), wo_ref[ATTN_WIDTH:, :], preferred_element_type=F32)
    y_ref[...] = x_ref[...] + _rms(mix, g_ref[...])


def _mix_out(x2d, attn, u, vg, ws_n, bsT_n, wo_b, g_post, *, n, tm):
    ntok = x2d.shape[0]
    tok = lambda w: pl.BlockSpec((tm, w), lambda i: (i, 0))
    return pl.pallas_call(
        functools.partial(_mix_out_kernel, n=n),
        grid=(ntok // tm,),
        in_specs=[tok(D_MODEL), tok(ATTN_WIDTH), tok(GMLP_WIDTH), tok(GMLP_WIDTH),
                  _const_spec((GMLP_GROUPS, n, n)), _const_spec((n, GMLP_GROUPS)),
                  _const_spec((ATTN_WIDTH + GMLP_WIDTH, D_MODEL)), _const_spec((1, D_MODEL))],
        out_specs=tok(D_MODEL),
        out_shape=jax.ShapeDtypeStruct((ntok, D_MODEL), F32),
        scratch_shapes=[pltpu.VMEM((tm, GMLP_WIDTH), F32)],
        compiler_params=_params(1),
        name="mix_out",
    )(x2d, attn, u, vg, ws_n, bsT_n, wo_b, g_post)


def _mem_kv_kernel(m_ref, g_ref, wk_ref, wv_ref, mk_ref, mv_ref):
    h = _rms(m_ref[...], g_ref[...]).astype(BF16)
    mk_ref[...] = jnp.dot(h, wk_ref[...], preferred_element_type=F32)
    mv_ref[...] = jnp.dot(h, wv_ref[...], preferred_element_type=F32)


def _mem_kv(mem2d, g, wk_b, wv_b):
    n = mem2d.shape[0]
    tm = min(512, n)
    tok = lambda w: pl.BlockSpec((tm, w), lambda i: (i, 0))
    return pl.pallas_call(
        _mem_kv_kernel,
        grid=(n // tm,),
        in_specs=[tok(D_MODEL), _const_spec((1, D_MODEL)), _const_spec((D_MODEL, MEM_WIDTH)),
                  _const_spec((D_MODEL, MEM_WIDTH))],
        out_specs=[tok(MEM_WIDTH), tok(MEM_WIDTH)],
        out_shape=[jax.ShapeDtypeStruct((n, MEM_WIDTH), F32)] * 2,
        compiler_params=_params(1),
        name="mem_kv",
    )(mem2d, g, wk_b, wv_b)


def _mem_attend_kernel(x_ref, mkT_ref, mv_ref, gpre_ref, wq_ref, wo_ref, gpost_ref, y_ref, o_ref):
    x = x_ref[...]
    h = _rms(x, gpre_ref[...]).astype(BF16)
    q = jnp.dot(h, wq_ref[...], preferred_element_type=F32).astype(BF16)
    for hd in range(MEM_HEADS):
        cols = slice(hd * MEM_HEAD_DIM, (hd + 1) * MEM_HEAD_DIM)
        lg = jnp.dot(q[:, cols], mkT_ref[cols, :], preferred_element_type=F32) * (MEM_HEAD_DIM ** -0.5)
        e = jnp.exp(lg - jnp.max(lg, axis=-1, keepdims=True))
        p = e / jnp.sum(e, axis=-1, keepdims=True)
        o_ref[:, cols] = jnp.dot(p.astype(BF16), mv_ref[:, cols], preferred_element_type=F32)
    out = jnp.dot(o_ref[...].astype(BF16), wo_ref[...], preferred_element_type=F32)
    y_ref[...] = x + _rms(out, gpost_ref[...])


def _mem_attend(x2d, mkT, mvb, g_pre, wq_b, wo_b, g_post, *, tm, tiles_per_batch):
    ntok = x2d.shape[0]
    n_mem = mvb.shape[1]
    tok = lambda w: pl.BlockSpec((tm, w), lambda i: (i, 0))
    return pl.pallas_call(
        _mem_attend_kernel,
        grid=(ntok // tm,),
        in_specs=[tok(D_MODEL),
                  pl.BlockSpec((None, MEM_WIDTH, n_mem), lambda i: (i // tiles_per_batch, 0, 0)),
                  pl.BlockSpec((None, n_mem, MEM_WIDTH), lambda i: (i // tiles_per_batch, 0, 0)),
                  _const_spec((1, D_MODEL)), _const_spec((D_MODEL, MEM_WIDTH)),
                  _const_spec((MEM_WIDTH, D_MODEL)), _const_spec((1, D_MODEL))],
        out_specs=tok(D_MODEL),
        out_shape=jax.ShapeDtypeStruct((ntok, D_MODEL), F32),
        scratch_shapes=[pltpu.VMEM((tm, MEM_WIDTH), F32)],
        compiler_params=_params(1),
        name="mem_attend",
    )(x2d, mkT, mvb, g_pre, wq_b, wo_b, g_post)


FF_CHUNK = 1024


def _ffn_kernel(x_ref, gpre_ref, w1_ref, w2_ref, gpost_ref, y_ref):
    x = x_ref[...]
    h = _rms(x, gpre_ref[...]).astype(BF16)
    d_ff = w1_ref.shape[1]
    out = jnp.zeros(x.shape, F32)
    for c in range(d_ff // FF_CHUNK):
        cols = slice(c * FF_CHUNK, (c + 1) * FF_CHUNK)
        a = jnp.maximum(jnp.dot(h, w1_ref[:, cols], preferred_element_type=F32), 0.0)
        out = out + jnp.dot((a * a).astype(BF16), w2_ref[cols, :], preferred_element_type=F32)
    y_ref[...] = x + _rms(out, gpost_ref[...])


def _ffn(x2d, g_pre, w1_b, w2_b, g_post):
    ntok = x2d.shape[0]
    tm = min(512, ntok)
    d_ff = w1_b.shape[1]
    tok = pl.BlockSpec((tm, D_MODEL), lambda i: (i, 0))
    return pl.pallas_call(
        _ffn_kernel,
        grid=(ntok // tm,),
        in_specs=[tok, _const_spec((1, D_MODEL)), _const_spec((D_MODEL, d_ff)), _const_spec((d_ff, D_MODEL)),
                  _const_spec((1, D_MODEL))],
        out_specs=tok,
        out_shape=jax.ShapeDtypeStruct((ntok, D_MODEL), F32),
        compiler_params=_params(1),
        name="ffn",
    )(x2d, g_pre, w1_b, w2_b, g_post)


def _pad_axis(a, axis, size):
    pad = [(0, 0)] * a.ndim
    pad[axis] = (0, size - a.shape[axis])
    return jnp.pad(a, pad)


def _augment_vT(vb3):
    b, s, _ = vb3.shape
    vT = jnp.transpose(vb3.reshape(b, s, N_HEADS, HEAD_DIM), (2, 3, 0, 1))
    ones = jnp.ones((N_HEADS, 1, b, s), BF16)
    zeros = jnp.zeros((N_HEADS, V_ROWS - HEAD_DIM - 1, b, s), BF16)
    return jnp.concatenate([vT, ones, zeros], axis=1).reshape(N_HEADS * V_ROWS, b * s)


def _group(x, w, *, past=None, mem_kv=None):
    b, t, _ = x.shape
    x2d = x.reshape(b * t, D_MODEL)
    causal = past is None
    k, kb, v, ikw, ikb, u, vg, *dsa_in = _proj(x2d, w["g_mix_pre"], w["w_pad"], w["ln_g"], w["ln_b"],
                                              transposed=causal)
    ik = ikw[:, :IDX_DIM]
    if causal:
        qT, iqT, iwT, vT_aug = dsa_in
        n_keys = t
        nq = NQ_PROMPT
    else:
        nq = NQ_CHUNK
        q, iq, vb = dsa_in
        ck, cv, cik = past
        n_keys = ck.shape[1] + t
        s_pad = -(-n_keys // (2 * ST)) * (2 * ST)
        tq = -(-t // nq) * nq
        per_stream = lambda a: a.reshape(b, t, a.shape[-1])
        keys = lambda old, new: _pad_axis(jnp.concatenate([old.astype(BF16), per_stream(new)], axis=1), 1, s_pad)
        kb = keys(ck, kb).reshape(b * s_pad, ATTN_WIDTH)
        ikb = keys(_pad_axis(cik, 2, LANES), ikb).reshape(b * s_pad, LANES)
        vT_aug = _augment_vT(keys(cv, vb))
        queries = lambda a: jnp.transpose(_pad_axis(per_stream(a), 1, tq), (2, 0, 1)).reshape(a.shape[-1], b * tq)
        qT, iqT, iwT = queries(q), queries(iq), queries(ikw[:, _IW_LO:_IW_HI])
    topk = min(TOPK_MAX, n_keys // 4)
    attn = _dsa(qT, iqT, iwT, ikb, kb, vT_aug, n_streams=b, nq=nq, causal=causal, n_keys=n_keys,
                nq_valid=min(t, nq), topk=topk)
    attn2d = attn.reshape(b, -1, ATTN_WIDTH)[:, :t].reshape(b * t, ATTN_WIDTH)

    n = min(t, GMLP_CHUNK)
    tm = min(512, b * t)
    x2d = _mix_out(x2d, attn2d, u, vg, w["w_s"][:, :n, :n], jnp.swapaxes(w["b_s"][:, :n], 0, 1),
                   w["w_o"], w["g_mix_post"], n=n, tm=tm)
    mk, mv = mem_kv
    tm_mem = min(512, t)
    x2d = _mem_attend(x2d, jnp.swapaxes(mk, 1, 2).astype(BF16), mv.astype(BF16), w["g_mem_pre"], w["w_mq"],
                      w["w_mo"], w["g_mem_post"], tm=tm_mem, tiles_per_batch=t // tm_mem)
    x2d = _ffn(x2d, w["g_ffn_pre"], w["w1"], w["w2"], w["g_ffn_post"])
    new = (k.reshape(b, t, N_HEADS, HEAD_DIM), v.reshape(b, t, N_HEADS, HEAD_DIM), ik.reshape(b, t, IDX_DIM),
           vg.reshape(b, t, GMLP_GROUPS, GMLP_GROUP_DIM))
    return x2d.reshape(b, t, D_MODEL), new


def kernel(x_prompt, x_sample, cache_attn_k, cache_attn_v, cache_idx_k, cache_mem_k, cache_mem_v, mem_prompt, g_mix_pre, w_in, ln_gmlp_g, ln_gmlp_b, w_s, b_s, w_o, g_mix_post, g_mem_kv, w_mk, w_mv, g_mem_pre, w_mq, w_mo, g_mem_post, g_ffn_pre, w1, w2, g_ffn_post):
    depth = w_in.shape[0]
    xp, xs = x_prompt, x_sample
    bp, n_mem = mem_prompt.shape[0], mem_prompt.shape[1]
    bs = xs.shape[0]
    row = lambda a: a.reshape(1, -1)
    outs = [[] for _ in range(9)]
    for l in range(depth):
        split = _IKW0 + _IW_HI
        w_pad = jnp.concatenate([w_in[l][:, :split], jnp.zeros((D_MODEL, _ZU0 - split), F32), w_in[l][:, split:]],
                                axis=1).astype(BF16)
        w = dict(g_mix_pre=row(g_mix_pre[l]), w_pad=w_pad, ln_g=row(ln_gmlp_g[l]), ln_b=row(ln_gmlp_b[l]),
                 w_s=w_s[l], b_s=b_s[l], w_o=w_o[l].astype(BF16), g_mix_post=row(g_mix_post[l]),
                 g_mem_pre=row(g_mem_pre[l]), w_mq=w_mq[l].astype(BF16), w_mo=w_mo[l].astype(BF16),
                 g_mem_post=row(g_mem_post[l]), g_ffn_pre=row(g_ffn_pre[l]), w1=w1[l].astype(BF16),
                 w2=w2[l].astype(BF16), g_ffn_post=row(g_ffn_post[l]))
        mk, mv = _mem_kv(mem_prompt.reshape(bp * n_mem, D_MODEL), row(g_mem_kv[l]), w_mk[l].astype(BF16),
                         w_mv[l].astype(BF16))
        mk, mv = mk.reshape(bp, n_mem, MEM_WIDTH), mv.reshape(bp, n_mem, MEM_WIDTH)
        xp, (kp, vp, ikp, _) = _group(xp, w, mem_kv=(mk, mv))
        past = (cache_attn_k[l].reshape(bs, -1, ATTN_WIDTH), cache_attn_v[l].reshape(bs, -1, ATTN_WIDTH),
                cache_idx_k[l])
        cmk = cache_mem_k[l].reshape(bs, -1, MEM_WIDTH)
        cmv = cache_mem_v[l].reshape(bs, -1, MEM_WIDTH)
        xs, (ks, vs, iks, gvs) = _group(xs, w, past=past, mem_kv=(cmk, cmv))
        new = (kp, vp, ikp, mk.reshape(bp, n_mem, MEM_HEADS, MEM_HEAD_DIM),
               mv.reshape(bp, n_mem, MEM_HEADS, MEM_HEAD_DIM), ks, vs, iks, gvs)
        for lst, a in zip(outs, new):
            lst.append(a)
    return (xp, xs) + tuple(jnp.stack(lst) for lst in outs)
```

```python
import functools
import math

import jax
import jax.numpy as jnp
from jax import lax
from jax.experimental import pallas as pl
from jax.experimental.pallas import tpu as pltpu

F32 = jnp.float32
BF16 = jnp.bfloat16

D_MODEL = 1024
N_HEADS = 8
HEAD_DIM = 64
ATTN_WIDTH = N_HEADS * HEAD_DIM
N_IDX_HEADS = 8
IDX_DIM = 64
GMLP_GROUPS = 4
GMLP_GROUP_DIM = 128
GMLP_WIDTH = GMLP_GROUPS * GMLP_GROUP_DIM
GMLP_CHUNK = 128
CHUNK = 64
TOPK_MAX = 256
MEM_HEADS = 4
MEM_HEAD_DIM = 128
MEM_WIDTH = MEM_HEADS * MEM_HEAD_DIM
EPS = 1e-6
LN_EPS = 1e-5

LANES = 128
SUBLANES = 8
VMEM_LIMIT = 56 * 1024 * 1024

_Q0, _K0, _V0, _IQ0, _IKW0, _ZU0, _ZV0, _WPAD = 0, 512, 1024, 1536, 2048, 2176, 2688, 3200
_IW_LO, _IW_HI = IDX_DIM, IDX_DIM + N_IDX_HEADS

NQ_PROMPT = 256
NQ_CHUNK = 128
KT = 256
ST = 512
V_ROWS = 80
ACC_CHAINS = 4
COUNT_ROWS = 128
FINE_BITS = 17
FINE_SPAN = 2 ** FINE_BITS
FINE_UNTESTED = 7


def _rms(x, g):
    return x * lax.rsqrt(jnp.mean(x * x, axis=-1, keepdims=True) + EPS) * g


def _params(n_axes):
    return pltpu.CompilerParams(dimension_semantics=("arbitrary",) * n_axes, vmem_limit_bytes=VMEM_LIMIT)


def _const_spec(shape):
    return pl.BlockSpec(shape, lambda *_: (0,) * len(shape), pipeline_mode=pl.Buffered(1))


def _proj_kernel(x_ref, g_ref, w_ref, lng_ref, lnb_ref, *out_refs, transposed):
    k_ref, kb_ref, v_ref, ikw_ref, ikb_ref, u_ref, vg_ref = out_refs[:7]
    h = _rms(x_ref[...], g_ref[...]).astype(BF16)

    def seg(lo, hi):
        return jnp.dot(h, w_ref[:, lo:hi], preferred_element_type=F32)

    q = seg(_Q0, _K0) * (HEAD_DIM ** -0.5 * math.log2(math.e))
    k = seg(_K0, _V0)
    k_ref[...] = k
    kb_ref[...] = k.astype(BF16)
    v = seg(_V0, _IQ0)
    v_ref[...] = v
    iq = seg(_IQ0, _IKW0) * (IDX_DIM ** -0.5)
    ikw = seg(_IKW0, _ZU0)
    lane = lax.broadcasted_iota(jnp.int32, ikw.shape, 1)
    is_iw = (lane >= _IW_LO) & (lane < _IW_HI)
    ikw = jnp.where(is_iw, ikw * (N_IDX_HEADS ** -0.5), ikw)
    ikw_ref[...] = ikw
    ikb_ref[...] = ikw.astype(BF16)
    u_ref[...] = jax.nn.gelu(seg(_ZU0, _ZV0))
    gv = jax.nn.gelu(seg(_ZV0, _WPAD))
    mu = jnp.mean(gv, axis=-1, keepdims=True)
    var = jnp.mean(jnp.square(gv - mu), axis=-1, keepdims=True)
    vg_ref[...] = (gv - mu) * lax.rsqrt(var + LN_EPS) * lng_ref[...] + lnb_ref[...]

    if not transposed:
        q_ref, iq_ref, vb_ref = out_refs[7:]
        q_ref[...] = q.astype(BF16)
        iq_ref[...] = iq.astype(BF16)
        vb_ref[...] = v.astype(BF16)
        return
    qT_ref, iqT_ref, iwT_ref, vTa_ref = out_refs[7:]
    qT_ref[...] = q.T.astype(BF16)
    iqT_ref[...] = iq.T.astype(BF16)
    iwT_ref[...] = ikw.T[_IW_LO:_IW_HI, :]
    vT = v.T
    pad_row = lax.broadcasted_iota(jnp.int32, (V_ROWS - HEAD_DIM, vT.shape[1]), 0)
    ones_then_zeros = jnp.where(pad_row == 0, 1.0, 0.0).astype(BF16)
    for hd in range(N_HEADS):
        vTa_ref[hd * V_ROWS:hd * V_ROWS + HEAD_DIM, :] = vT[hd * HEAD_DIM:(hd + 1) * HEAD_DIM, :].astype(BF16)
        vTa_ref[hd * V_ROWS + HEAD_DIM:(hd + 1) * V_ROWS, :] = ones_then_zeros


def _proj(x2d, g_pre, w_pad, ln_g, ln_b, *, transposed):
    n = x2d.shape[0]
    tm = min(512, n)
    tok = lambda w: pl.BlockSpec((tm, w), lambda i: (i, 0))
    feat = lambda r: pl.BlockSpec((r, tm), lambda i: (0, i))
    outs = [(ATTN_WIDTH, F32), (ATTN_WIDTH, BF16), (ATTN_WIDTH, F32), (LANES, F32), (LANES, BF16),
            (GMLP_WIDTH, F32), (GMLP_WIDTH, F32)]
    out_specs = [tok(w) for w, _ in outs]
    out_shape = [jax.ShapeDtypeStruct((n, w), dt) for w, dt in outs]
    if transposed:
        extra = [(ATTN_WIDTH, BF16), (N_IDX_HEADS * IDX_DIM, BF16), (N_IDX_HEADS, F32), (N_HEADS * V_ROWS, BF16)]
        out_specs += [feat(r) for r, _ in extra]
        out_shape += [jax.ShapeDtypeStruct((r, n), dt) for r, dt in extra]
    else:
        extra = [(ATTN_WIDTH, BF16), (N_IDX_HEADS * IDX_DIM, BF16), (ATTN_WIDTH, BF16)]
        out_specs += [tok(w) for w, _ in extra]
        out_shape += [jax.ShapeDtypeStruct((n, w), dt) for w, dt in extra]
    return pl.pallas_call(
        functools.partial(_proj_kernel, transposed=transposed),
        grid=(n // tm,),
        in_specs=[tok(D_MODEL), _const_spec((1, D_MODEL)), _const_spec((D_MODEL, _WPAD)),
                  _const_spec((1, GMLP_WIDTH)), _const_spec((1, GMLP_WIDTH))],
        out_specs=out_specs,
        out_shape=out_shape,
        compiler_params=_params(1),
        name="proj",
    )(x2d, g_pre, w_pad, ln_g, ln_b)


def _key_to_float(u):
    bits = jnp.where(u < 0, u ^ jnp.int32(-2 ** 31), ~u)
    return lax.bitcast_convert_type(bits, F32)


def _fold(x, op):
    nq = x.shape[-1]
    part = op(x.reshape(-1, ACC_CHAINS * SUBLANES, nq), axis=0)
    return op(part.reshape(ACC_CHAINS, SUBLANES, nq), axis=0)


def _dsa_kernel(qT_ref, iqT_ref, iwT_ref, ik_ref, k_ref, vT_ref, o_ref,
                score_ref, rscore_ref, widx_ref, wq_ref, s_ref, acc_ref, outT_ref, mt_ref,
                *, nq, causal, n_keys, s_pad, nq_valid, topk):
    j = pl.program_id(1)
    lane = lax.broadcasted_iota(jnp.int32, (1, nq), 1)
    if causal:
        pos = j * nq + lane
        lim = (lax.shift_right_logical(pos, CHUNK.bit_length() - 1) + 1) * CHUNK
        nt = ((j + 1) * nq + (ST - 1)) // ST
        nt_att = ((nt + 1) // 2) * 2
    else:
        lim = jnp.full((1, nq), n_keys, jnp.int32)
        nt = nt_att = s_pad // ST
    neg_inf = jnp.float32(-jnp.inf)
    key_row = lax.broadcasted_iota(jnp.int32, (KT, nq), 0)

    def tile(t):
        return pl.ds(pl.multiple_of(t * ST, ST), ST)

    widx_ref[IDX_DIM:, :] = jnp.zeros((LANES - IDX_DIM, N_IDX_HEADS * nq), BF16)
    for h in range(N_IDX_HEADS):
        widx_ref[:IDX_DIM, h * nq:(h + 1) * nq] = iqT_ref[h * IDX_DIM:(h + 1) * IDX_DIM, :]
    pair_row = lax.broadcasted_iota(jnp.int32, (2 * HEAD_DIM, nq), 0)
    for p in range(N_HEADS // 2):
        blk = qT_ref[p * 2 * HEAD_DIM:(p + 1) * 2 * HEAD_DIM, :]
        zero = jnp.zeros_like(blk)
        wq_ref[p, :, 0:nq] = jnp.where(pair_row < HEAD_DIM, blk, zero)
        wq_ref[p, :, nq:2 * nq] = jnp.where(pair_row >= HEAD_DIM, blk, zero)

    def idx_body(t, carry):
        for sub in range(ST // KT):
            r0 = pl.multiple_of(t * ST + sub * KT, KT)
            logits = jnp.dot(ik_ref[pl.ds(r0, KT), :], widx_ref[...], preferred_element_type=F32)
            sc = jnp.zeros((KT, nq), F32)
            for h in range(N_IDX_HEADS):
                sc = sc + jnp.maximum(logits[:, h * nq:(h + 1) * nq], 0.0) * iwT_ref[h:h + 1, :]
            sc = jnp.where(r0 + key_row < lim, sc, neg_inf)
            score_ref[pl.ds(r0, KT), :] = sc
            rscore_ref[pl.ds(r0, KT), :] = sc.astype(BF16)
        return carry

    lax.fori_loop(0, nt, idx_body, 0)

    def count(indicator):
        def body(t, acc):
            for part in range(ST // COUNT_ROWS):
                r0 = pl.multiple_of(t * ST + part * COUNT_ROWS, COUNT_ROWS)
                acc = acc + _fold(indicator(score_ref[pl.ds(r0, COUNT_ROWS), :]), jnp.sum)
            return acc
        acc = lax.fori_loop(0, nt, body, jnp.zeros((SUBLANES, nq), F32))
        return jnp.sum(acc, axis=0, keepdims=True)

    def count_rounded(cand):
        cb = jnp.broadcast_to(cand, (2 * SUBLANES, nq)).astype(BF16)

        def body(t, acc):
            xb = rscore_ref[tile(t), :].reshape(ST // (2 * SUBLANES), 2 * SUBLANES, nq)
            ind = jnp.where(xb < cb[None], jnp.zeros((), BF16), jnp.ones((), BF16))
            for chain in range(ACC_CHAINS):
                part = ind[chain]
                for r in range(chain + ACC_CHAINS, ind.shape[0], ACC_CHAINS):
                    part = part + ind[r]
                acc = acc + part.astype(F32)
            return acc
        acc = lax.fori_loop(0, nt, body, jnp.zeros((2 * SUBLANES, nq), F32))
        return jnp.sum(acc, axis=0, keepdims=True)

    def bf16_key_to_key(u16):
        return lax.shift_left(u16, 16) | jnp.where(u16 < 2 ** 15, 2 ** 16 - 1, 0)

    def coarse_body(i, u16):
        trial = u16 | lax.shift_left(jnp.int32(1), 15 - i)
        c = count_rounded(_key_to_float(bf16_key_to_key(trial)))
        return jnp.where(c >= topk, trial, u16)

    u16 = lax.fori_loop(0, 16, coarse_body, jnp.zeros((1, nq), jnp.int32))
    base = bf16_key_to_key(u16) - FINE_SPAN // 2

    def fine_step(i, delta, settled):
        trial = delta | lax.shift_left(jnp.int32(1), FINE_BITS - 1 - i)
        cand = _key_to_float(base + trial)
        c = count(lambda x: jnp.where(x < cand, 0.0, 1.0))
        delta = jnp.where(settled > 0.0, delta, jnp.where(c >= topk, trial, delta))
        return delta, jnp.where(c == topk, 1.0, settled)

    def unsettled(settled):
        return jnp.min(settled) == 0.0

    def fine_body(carry):
        i, delta, settled, _ = carry
        delta, settled = fine_step(i, delta, settled)
        delta, settled = fine_step(i + 1, delta, settled)
        return i + 2, delta, settled, unsettled(settled)

    delta, settled = lax.fori_loop(0, FINE_UNTESTED, lambda i, carry: fine_step(i, *carry),
                                   (jnp.zeros((1, nq), jnp.int32), jnp.where(lane >= nq_valid, 1.0, 0.0)))
    _, delta, settled, open_lanes = lax.while_loop(lambda carry: (carry[0] < FINE_BITS) & carry[3], fine_body,
                                                   (jnp.int32(FINE_UNTESTED), delta, settled, unsettled(settled)))
    prefix = base + delta
    thr = jnp.maximum(_key_to_float(prefix), jnp.finfo(F32).min)

    @pl.when(jnp.logical_not(open_lanes))
    def _():
        def body(t, carry):
            score_ref[tile(t), :] = jnp.where(score_ref[tile(t), :] >= thr, 0.0, neg_inf)
            return carry
        lax.fori_loop(0, nt, body, 0)

    @pl.when(open_lanes)
    def _():
        need = topk - count(lambda x: jnp.where(x > thr, 1.0, 0.0))
        r = lax.broadcasted_iota(jnp.int32, (ST, ST), 0)
        c = lax.broadcasted_iota(jnp.int32, (ST, ST), 1)
        lower = jnp.where(c <= r, 1.0, 0.0).astype(BF16)

        def body(t, seen):
            x = score_ref[tile(t), :]
            tied = jnp.where(x == thr, 1.0, 0.0)
            rank = seen + jnp.dot(lower, tied.astype(BF16), preferred_element_type=F32)
            keep = jnp.where(rank <= need, 0.0, neg_inf)
            score_ref[tile(t), :] = jnp.where(x > thr, 0.0, jnp.where(x == thr, keep, neg_inf))
            return rank[ST - 1:ST, :]
        lax.fori_loop(0, nt, body, jnp.zeros((1, nq), F32))

    if causal:
        @pl.when(nt_att > nt)
        def _():
            score_ref[tile(nt), :] = jnp.full((ST, nq), neg_inf, F32)

    acc_ref[...] = jnp.zeros(acc_ref.shape, F32)

    def logits(t, slot):
        for p in range(N_HEADS // 2):
            lp = jnp.dot(k_ref[tile(t), p * 2 * HEAD_DIM:(p + 1) * 2 * HEAD_DIM], wq_ref[p],
                         preferred_element_type=F32)
            for hh in range(2):
                h = 2 * p + hh
                s = lp[:, hh * nq:(hh + 1) * nq] + score_ref[tile(t), :]
                s_ref[slot, h] = s
                mt_ref[slot, h:h + 1, :] = jnp.max(_fold(s, jnp.max), axis=0, keepdims=True)
        return mt_ref[slot]

    def rescale(m_old, mt):
        m_new = jnp.maximum(m_old, mt)
        m_safe = jnp.where(m_new == neg_inf, 0.0, m_new)
        return m_new, m_safe, jnp.exp2(m_old - m_safe)

    def accumulate(t, slot, m_safe, alpha):
        for h in range(N_HEADS):
            rows = slice(h * V_ROWS, (h + 1) * V_ROWS)
            pr = jnp.exp2(s_ref[slot, h] - m_safe[h:h + 1, :]).astype(BF16)
            pv = jnp.dot(vT_ref[rows, tile(t)], pr, preferred_element_type=F32)
            acc_ref[rows, :] = alpha[h:h + 1, :] * acc_ref[rows, :] + pv

    state = rescale(jnp.full((N_HEADS, nq), neg_inf, F32), logits(0, 0))

    def att_body(u, state):
        m, m_safe, alpha = state
        mt = logits(2 * u + 1, 1)
        accumulate(2 * u, 0, m_safe, alpha)
        m, m_safe, alpha = rescale(m, mt)
        mt = logits(2 * u + 2, 0)
        accumulate(2 * u + 1, 1, m_safe, alpha)
        return rescale(m, mt)

    m, m_safe, alpha = lax.fori_loop(0, nt_att // 2 - 1, att_body, state)
    mt = logits(nt_att - 1, 1)
    accumulate(nt_att - 2, 0, m_safe, alpha)
    m, m_safe, alpha = rescale(m, mt)
    accumulate(nt_att - 1, 1, m_safe, alpha)

    for h in range(N_HEADS):
        den = acc_ref[h * V_ROWS + HEAD_DIM:h * V_ROWS + HEAD_DIM + 1, :]
        outT_ref[h * HEAD_DIM:(h + 1) * HEAD_DIM, :] = acc_ref[h * V_ROWS:h * V_ROWS + HEAD_DIM, :] * (1.0 / den)
    o_ref[...] = outT_ref[...].T


def _dsa(qT, iqT, iwT, ikb, kb, vT_aug, *, n_streams, nq, causal, n_keys, nq_valid, topk):
    s_pad = kb.shape[0] // n_streams
    n_qblk = qT.shape[1] // (n_streams * nq)
    kern = functools.partial(_dsa_kernel, nq=nq, causal=causal, n_keys=n_keys, s_pad=s_pad, nq_valid=nq_valid,
                             topk=topk)
    per_q = lambda rows: pl.BlockSpec((rows, nq), lambda bi, j: (0, bi * n_qblk + j))
    one_buf = pl.Buffered(1)
    return pl.pallas_call(
        kern,
        grid=(n_streams, n_qblk),
        in_specs=[per_q(ATTN_WIDTH), per_q(N_IDX_HEADS * IDX_DIM), per_q(N_IDX_HEADS),
                  pl.BlockSpec((s_pad, LANES), lambda bi, j: (bi, 0), pipeline_mode=one_buf),
                  pl.BlockSpec((s_pad, ATTN_WIDTH), lambda bi, j: (bi, 0), pipeline_mode=one_buf),
                  pl.BlockSpec((N_HEADS * V_ROWS, s_pad), lambda bi, j: (0, bi), pipeline_mode=one_buf)],
        out_specs=pl.BlockSpec((nq, ATTN_WIDTH), lambda bi, j: (bi * n_qblk + j, 0)),
        out_shape=jax.ShapeDtypeStruct((n_streams * n_qblk * nq, ATTN_WIDTH), F32),
        scratch_shapes=[pltpu.VMEM((s_pad, nq), F32),
                        pltpu.VMEM((s_pad, nq), BF16),
                        pltpu.VMEM((LANES, N_IDX_HEADS * nq), BF16),
                        pltpu.VMEM((N_HEADS // 2, 2 * HEAD_DIM, 2 * nq), BF16),
                        pltpu.VMEM((2, N_HEADS, ST, nq), F32),
                        pltpu.VMEM((N_HEADS * V_ROWS, nq), F32),
                        pltpu.VMEM((ATTN_WIDTH, nq), F32),
                        pltpu.VMEM((2, N_HEADS, nq), F32)],
        compiler_params=_params(2),
        name="dsa",
    )(qT, iqT, iwT, ikb, kb, vT_aug)


def _mix_out_kernel(x_ref, a_ref, u_ref, vg_ref, ws_ref, bsT_ref, wo_ref, g_ref, y_ref, gated_ref, *, n):
    tm = x_ref.shape[0]
    r = lax.broadcasted_iota(jnp.int32, (n, n), 0)
    c = lax.broadcasted_iota(jnp.int32, (n, n), 1)
    for g in range(GMLP_GROUPS):
        cols = slice(g * GMLP_GROUP_DIM, (g + 1) * GMLP_GROUP_DIM)
        w = jnp.where(c <= r, ws_ref[g], 0.0).astype(BF16)
        bias = bsT_ref[:, g:g + 1]
        for ch in range(tm // n):
            rows = slice(ch * n, (ch + 1) * n)
            s = jnp.dot(w, vg_ref[rows, cols].astype(BF16), preferred_element_type=F32) + bias
            gated_ref[rows, cols] = u_ref[rows, cols] * s
    mix = jnp.dot(a_ref[...].astype(BF16), wo_ref[:ATTN_WIDTH, :], preferred_element_type=F32)
    mix = mix + jnp.dot(gated_ref[...].astype(BF16), wo_ref[ATTN_WIDTH:, :], preferred_element_type=F32)
    y_ref[...] = x_ref[...] + _rms(mix, g_ref[...])


def _mix_out(x2d, attn, u, vg, ws_n, bsT_n, wo_b, g_post, *, n, tm):
    ntok = x2d.shape[0]
    tok = lambda w: pl.BlockSpec((tm, w), lambda i: (i, 0))
    return pl.pallas_call(
        functools.partial(_mix_out_kernel, n=n),
        grid=(ntok // tm,),
        in_specs=[tok(D_MODEL), tok(ATTN_WIDTH), tok(GMLP_WIDTH), tok(GMLP_WIDTH),
                  _const_spec((GMLP_GROUPS, n, n)), _const_spec((n, GMLP_GROUPS)),
                  _const_spec((ATTN_WIDTH + GMLP_WIDTH, D_MODEL)), _const_spec((1, D_MODEL))],
        out_specs=tok(D_MODEL),
        out_shape=jax.ShapeDtypeStruct((ntok, D_MODEL), F32),
        scratch_shapes=[pltpu.VMEM((tm, GMLP_WIDTH), F32)],
        compiler_params=_params(1),
        name="mix_out",
    )(x2d, attn, u, vg, ws_n, bsT_n, wo_b, g_post)


def _mem_kv_kernel(m_ref, g_ref, wk_ref, wv_ref, mk_ref, mv_ref):
    h = _rms(m_ref[...], g_ref[...]).astype(BF16)
    mk_ref[...] = jnp.dot(h, wk_ref[...], preferred_element_type=F32)
    mv_ref[...] = jnp.dot(h, wv_ref[...], preferred_element_type=F32)


def _mem_kv(mem2d, g, wk_b, wv_b):
    n = mem2d.shape[0]
    tm = min(512, n)
    tok = lambda w: pl.BlockSpec((tm, w), lambda i: (i, 0))
    return pl.pallas_call(
        _mem_kv_kernel,
        grid=(n // tm,),
        in_specs=[tok(D_MODEL), _const_spec((1, D_MODEL)), _const_spec((D_MODEL, MEM_WIDTH)),
                  _const_spec((D_MODEL, MEM_WIDTH))],
        out_specs=[tok(MEM_WIDTH), tok(MEM_WIDTH)],
        out_shape=[jax.ShapeDtypeStruct((n, MEM_WIDTH), F32)] * 2,
        compiler_params=_params(1),
        name="mem_kv",
    )(mem2d, g, wk_b, wv_b)


def _mem_attend_kernel(x_ref, mkT_ref, mv_ref, gpre_ref, wq_ref, wo_ref, gpost_ref, y_ref, o_ref):
    x = x_ref[...]
    h = _rms(x, gpre_ref[...]).astype(BF16)
    q = jnp.dot(h, wq_ref[...], preferred_element_type=F32).astype(BF16)
    for hd in range(MEM_HEADS):
        cols = slice(hd * MEM_HEAD_DIM, (hd + 1) * MEM_HEAD_DIM)
        lg = jnp.dot(q[:, cols], mkT_ref[cols, :], preferred_element_type=F32) * (MEM_HEAD_DIM ** -0.5)
        e = jnp.exp(lg - jnp.max(lg, axis=-1, keepdims=True))
        p = e / jnp.sum(e, axis=-1, keepdims=True)
        o_ref[:, cols] = jnp.dot(p.astype(BF16), mv_ref[:, cols], preferred_element_type=F32)
    out = jnp.dot(o_ref[...].astype(BF16), wo_ref[...], preferred_element_type=F32)
    y_ref[...] = x + _rms(out, gpost_ref[...])


def _mem_attend(x2d, mkT, mvb, g_pre, wq_b, wo_b, g_post, *, tm, tiles_per_batch):
    ntok = x2d.shape[0]
    n_mem = mvb.shape[1]
    tok = lambda w: pl.BlockSpec((tm, w), lambda i: (i, 0))
    return pl.pallas_call(
        _mem_attend_kernel,
        grid=(ntok // tm,),
        in_specs=[tok(D_MODEL),
                  pl.BlockSpec((None, MEM_WIDTH, n_mem), lambda i: (i // tiles_per_batch, 0, 0)),
                  pl.BlockSpec((None, n_mem, MEM_WIDTH), lambda i: (i // tiles_per_batch, 0, 0)),
                  _const_spec((1, D_MODEL)), _const_spec((D_MODEL, MEM_WIDTH)),
                  _const_spec((MEM_WIDTH, D_MODEL)), _const_spec((1, D_MODEL))],
        out_specs=tok(D_MODEL),
        out_shape=jax.ShapeDtypeStruct((ntok, D_MODEL), F32),
        scratch_shapes=[pltpu.VMEM((tm, MEM_WIDTH), F32)],
        compiler_params=_params(1),
        name="mem_attend",
    )(x2d, mkT, mvb, g_pre, wq_b, wo_b, g_post)


FF_CHUNK = 1024


def _ffn_kernel(x_ref, gpre_ref, w1_ref, w2_ref, gpost_ref, y_ref):
    x = x_ref[...]
    h = _rms(x, gpre_ref[...]).astype(BF16)
    d_ff = w1_ref.shape[1]
    out = jnp.zeros(x.shape, F32)
    for c in range(d_ff // FF_CHUNK):
        cols = slice(c * FF_CHUNK, (c + 1) * FF_CHUNK)
        a = jnp.maximum(jnp.dot(h, w1_ref[:, cols], preferred_element_type=F32), 0.0)
        out = out + jnp.dot((a * a).astype(BF16), w2_ref[cols, :], preferred_element_type=F32)
    y_ref[...] = x + _rms(out, gpost_ref[...])


def _ffn(x2d, g_pre, w1_b, w2_b, g_post):
    ntok = x2d.shape[0]
    tm = min(512, ntok)
    d_ff = w1_b.shape[1]
    tok = pl.BlockSpec((tm, D_MODEL), lambda i: (i, 0))
    return pl.pallas_call(
        _ffn_kernel,
        grid=(ntok // tm,),
        in_specs=[tok, _const_spec((1, D_MODEL)), _const_spec((D_MODEL, d_ff)), _const_spec((d_ff, D_MODEL)),
                  _const_spec((1, D_MODEL))],
        out_specs=tok,
        out_shape=jax.ShapeDtypeStruct((ntok, D_MODEL), F32),
        compiler_params=_params(1),
        name="ffn",
    )(x2d, g_pre, w1_b, w2_b, g_post)


def _pad_axis(a, axis, size):
    pad = [(0, 0)] * a.ndim
    pad[axis] = (0, size - a.shape[axis])
    return jnp.pad(a, pad)


def _augment_vT(vb3):
    b, s, _ = vb3.shape
    vT = jnp.transpose(vb3.reshape(b, s, N_HEADS, HEAD_DIM), (2, 3, 0, 1))
    ones = jnp.ones((N_HEADS, 1, b, s), BF16)
    zeros = jnp.zeros((N_HEADS, V_ROWS - HEAD_DIM - 1, b, s), BF16)
    return jnp.concatenate([vT, ones, zeros], axis=1).reshape(N_HEADS * V_ROWS, b * s)


def _group(x, w, *, past=None, mem_kv=None):
    b, t, _ = x.shape
    x2d = x.reshape(b * t, D_MODEL)
    causal = past is None
    k, kb, v, ikw, ikb, u, vg, *dsa_in = _proj(x2d, w["g_mix_pre"], w["w_pad"], w["ln_g"], w["ln_b"],
                                              transposed=causal)
    ik = ikw[:, :IDX_DIM]
    if causal:
        qT, iqT, iwT, vT_aug = dsa_in
        n_keys = t
        nq = NQ_PROMPT
    else:
        nq = NQ_CHUNK
        q, iq, vb = dsa_in
        ck, cv, cik = past
        n_keys = ck.shape[1] + t
        s_pad = -(-n_keys // (2 * ST)) * (2 * ST)
        tq = -(-t // nq) * nq
        per_stream = lambda a: a.reshape(b, t, a.shape[-1])
        keys = lambda old, new: _pad_axis(jnp.concatenate([old.astype(BF16), per_stream(new)], axis=1), 1, s_pad)
        kb = keys(ck, kb).reshape(b * s_pad, ATTN_WIDTH)
        ikb = keys(_pad_axis(cik, 2, LANES), ikb).reshape(b * s_pad, LANES)
        vT_aug = _augment_vT(keys(cv, vb))
        queries = lambda a: jnp.transpose(_pad_axis(per_stream(a), 1, tq), (2, 0, 1)).reshape(a.shape[-1], b * tq)
        qT, iqT, iwT = queries(q), queries(iq), queries(ikw[:, _IW_LO:_IW_HI])
    topk = min(TOPK_MAX, n_keys // 4)
    attn = _dsa(qT, iqT, iwT, ikb, kb, vT_aug, n_streams=b, nq=nq, causal=causal, n_keys=n_keys,
                nq_valid=min(t, nq), topk=topk)
    attn2d = attn.reshape(b, -1, ATTN_WIDTH)[:, :t].reshape(b * t, ATTN_WIDTH)

    n = min(t, GMLP_CHUNK)
    tm = min(512, b * t)
    x2d = _mix_out(x2d, attn2d, u, vg, w["w_s"][:, :n, :n], jnp.swapaxes(w["b_s"][:, :n], 0, 1),
                   w["w_o"], w["g_mix_post"], n=n, tm=tm)
    mk, mv = mem_kv
    tm_mem = min(512, t)
    x2d = _mem_attend(x2d, jnp.swapaxes(mk, 1, 2).astype(BF16), mv.astype(BF16), w["g_mem_pre"], w["w_mq"],
                      w["w_mo"], w["g_mem_post"], tm=tm_mem, tiles_per_batch=t // tm_mem)
    x2d = _ffn(x2d, w["g_ffn_pre"], w["w1"], w["w2"], w["g_ffn_post"])
    new = (k.reshape(b, t, N_HEADS, HEAD_DIM), v.reshape(b, t, N_HEADS, HEAD_DIM), ik.reshape(b, t, IDX_DIM),
           vg.reshape(b, t, GMLP_GROUPS, GMLP_GROUP_DIM))
    return x2d.reshape(b, t, D_MODEL), new


def kernel(x_prompt, x_sample, cache_attn_k, cache_attn_v, cache_idx_k, cache_mem_k, cache_mem_v, mem_prompt, g_mix_pre, w_in, ln_gmlp_g, ln_gmlp_b, w_s, b_s, w_o, g_mix_post, g_mem_kv, w_mk, w_mv, g_mem_pre, w_mq, w_mo, g_mem_post, g_ffn_pre, w1, w2, g_ffn_post):
    depth = w_in.shape[0]
    xp, xs = x_prompt, x_sample
    bp, n_mem = mem_prompt.shape[0], mem_prompt.shape[1]
    bs = xs.shape[0]
    row = lambda a: a.reshape(1, -1)
    outs = [[] for _ in range(9)]
    for l in range(depth):
        split = _IKW0 + _IW_HI
        w_pad = jnp.concatenate([w_in[l][:, :split], jnp.zeros((D_MODEL, _ZU0 - split), F32), w_in[l][:, split:]],
                                axis=1).astype(BF16)
        w = dict(g_mix_pre=row(g_mix_pre[l]), w_pad=w_pad, ln_g=row(ln_gmlp_g[l]), ln_b=row(ln_gmlp_b[l]),
                 w_s=w_s[l], b_s=b_s[l], w_o=w_o[l].astype(BF16), g_mix_post=row(g_mix_post[l]),
                 g_mem_pre=row(g_mem_pre[l]), w_mq=w_mq[l].astype(BF16), w_mo=w_mo[l].astype(BF16),
                 g_mem_post=row(g_mem_post[l]), g_ffn_pre=row(g_ffn_pre[l]), w1=w1[l].astype(BF16),
                 w2=w2[l].astype(BF16), g_ffn_post=row(g_ffn_post[l]))
        mk, mv = _mem_kv(mem_prompt.reshape(bp * n_mem, D_MODEL), row(g_mem_kv[l]), w_mk[l].astype(BF16),
                         w_mv[l].astype(BF16))
        mk, mv = mk.reshape(bp, n_mem, MEM_WIDTH), mv.reshape(bp, n_mem, MEM_WIDTH)
        xp, (kp, vp, ikp, _) = _group(xp, w, mem_kv=(mk, mv))
        past = (cache_attn_k[l].reshape(bs, -1, ATTN_WIDTH), cache_attn_v[l].reshape(bs, -1, ATTN_WIDTH),
                cache_idx_k[l])
        cmk = cache_mem_k[l].reshape(bs, -1, MEM_WIDTH)
        cmv = cache_mem_v[l].reshape(bs, -1, MEM_WIDTH)
        xs, (ks, vs, iks, gvs) = _group(xs, w, past=past, mem_kv=(cmk, cmv))
        new = (kp, vp, ikp, mk.reshape(bp, n_mem, MEM_HEADS, MEM_HEAD_DIM),
               mv.reshape(bp, n_mem, MEM_HEADS, MEM_HEAD_DIM), ks, vs, iks, gvs)
        for lst, a in zip(outs, new):
            lst.append(a)
    return (xp, xs) + tuple(jnp.stack(lst) for lst in outs)
```

```python
import functools
import math

import jax
import jax.numpy as jnp
from jax import lax
from jax.experimental import pallas as pl
from jax.experimental.pallas import tpu as pltpu

F32 = jnp.float32
BF16 = jnp.bfloat16

D_MODEL = 1024
N_HEADS = 8
HEAD_DIM = 64
ATTN_WIDTH = N_HEADS * HEAD_DIM
N_IDX_HEADS = 8
IDX_DIM = 64
GMLP_GROUPS = 4
GMLP_GROUP_DIM = 128
GMLP_WIDTH = GMLP_GROUPS * GMLP_GROUP_DIM
GMLP_CHUNK = 128
CHUNK = 64
TOPK_MAX = 256
MEM_HEADS = 4
MEM_HEAD_DIM = 128
MEM_WIDTH = MEM_HEADS * MEM_HEAD_DIM
EPS = 1e-6
LN_EPS = 1e-5

LANES = 128
SUBLANES = 8
VMEM_LIMIT = 56 * 1024 * 1024

_Q0, _K0, _V0, _IQ0, _IKW0, _ZU0, _ZV0, _WPAD = 0, 512, 1024, 1536, 2048, 2176, 2688, 3200
_IW_LO, _IW_HI = IDX_DIM, IDX_DIM + N_IDX_HEADS

NQ_PROMPT = 256
NQ_CHUNK = 128
KT = 256
ST = 512
V_ROWS = 80
ACC_CHAINS = 4
COUNT_ROWS = 128
FINE_BITS = 17
FINE_SPAN = 2 ** FINE_BITS
FINE_UNTESTED = 9


def _rms(x, g):
    return x * lax.rsqrt(jnp.mean(x * x, axis=-1, keepdims=True) + EPS) * g


def _params(n_axes):
    return pltpu.CompilerParams(dimension_semantics=("arbitrary",) * n_axes, vmem_limit_bytes=VMEM_LIMIT)


def _const_spec(shape):
    return pl.BlockSpec(shape, lambda *_: (0,) * len(shape), pipeline_mode=pl.Buffered(1))


def _proj_kernel(x_ref, g_ref, w_ref, lng_ref, lnb_ref, *out_refs, transposed):
    k_ref, kb_ref, v_ref, ikw_ref, ikb_ref, u_ref, vg_ref = out_refs[:7]
    h = _rms(x_ref[...], g_ref[...]).astype(BF16)

    def seg(lo, hi):
        return jnp.dot(h, w_ref[:, lo:hi], preferred_element_type=F32)

    q = seg(_Q0, _K0) * (HEAD_DIM ** -0.5 * math.log2(math.e))
    k = seg(_K0, _V0)
    k_ref[...] = k
    kb_ref[...] = k.astype(BF16)
    v = seg(_V0, _IQ0)
    v_ref[...] = v
    iq = seg(_IQ0, _IKW0) * (IDX_DIM ** -0.5)
    ikw = seg(_IKW0, _ZU0)
    lane = lax.broadcasted_iota(jnp.int32, ikw.shape, 1)
    is_iw = (lane >= _IW_LO) & (lane < _IW_HI)
    ikw = jnp.where(is_iw, ikw * (N_IDX_HEADS ** -0.5), ikw)
    ikw_ref[...] = ikw
    ikb_ref[...] = ikw.astype(BF16)
    u_ref[...] = jax.nn.gelu(seg(_ZU0, _ZV0))
    gv = jax.nn.gelu(seg(_ZV0, _WPAD))
    mu = jnp.mean(gv, axis=-1, keepdims=True)
    var = jnp.mean(jnp.square(gv - mu), axis=-1, keepdims=True)
    vg_ref[...] = (gv - mu) * lax.rsqrt(var + LN_EPS) * lng_ref[...] + lnb_ref[...]

    if not transposed:
        q_ref, iq_ref, vb_ref = out_refs[7:]
        q_ref[...] = q.astype(BF16)
        iq_ref[...] = iq.astype(BF16)
        vb_ref[...] = v.astype(BF16)
        return
    qT_ref, iqT_ref, iwT_ref, vTa_ref = out_refs[7:]
    qT_ref[...] = q.T.astype(BF16)
    iqT_ref[...] = iq.T.astype(BF16)
    iwT_ref[...] = ikw.T[_IW_LO:_IW_HI, :]
    vT = v.T
    pad_row = lax.broadcasted_iota(jnp.int32, (V_ROWS - HEAD_DIM, vT.shape[1]), 0)
    ones_then_zeros = jnp.where(pad_row == 0, 1.0, 0.0).astype(BF16)
    for hd in range(N_HEADS):
        vTa_ref[hd * V_ROWS:hd * V_ROWS + HEAD_DIM, :] = vT[hd * HEAD_DIM:(hd + 1) * HEAD_DIM, :].astype(BF16)
        vTa_ref[hd * V_ROWS + HEAD_DIM:(hd + 1) * V_ROWS, :] = ones_then_zeros


def _proj(x2d, g_pre, w_pad, ln_g, ln_b, *, transposed):
    n = x2d.shape[0]
    tm = min(512, n)
    tok = lambda w: pl.BlockSpec((tm, w), lambda i: (i, 0))
    feat = lambda r: pl.BlockSpec((r, tm), lambda i: (0, i))
    outs = [(ATTN_WIDTH, F32), (ATTN_WIDTH, BF16), (ATTN_WIDTH, F32), (LANES, F32), (LANES, BF16),
            (GMLP_WIDTH, F32), (GMLP_WIDTH, F32)]
    out_specs = [tok(w) for w, _ in outs]
    out_shape = [jax.ShapeDtypeStruct((n, w), dt) for w, dt in outs]
    if transposed:
        extra = [(ATTN_WIDTH, BF16), (N_IDX_HEADS * IDX_DIM, BF16), (N_IDX_HEADS, F32), (N_HEADS * V_ROWS, BF16)]
        out_specs += [feat(r) for r, _ in extra]
        out_shape += [jax.ShapeDtypeStruct((r, n), dt) for r, dt in extra]
    else:
        extra = [(ATTN_WIDTH, BF16), (N_IDX_HEADS * IDX_DIM, BF16), (ATTN_WIDTH, BF16)]
        out_specs += [tok(w) for w, _ in extra]
        out_shape += [jax.ShapeDtypeStruct((n, w), dt) for w, dt in extra]
    return pl.pallas_call(
        functools.partial(_proj_kernel, transposed=transposed),
        grid=(n // tm,),
        in_specs=[tok(D_MODEL), _const_spec((1, D_MODEL)), _const_spec((D_MODEL, _WPAD)),
                  _const_spec((1, GMLP_WIDTH)), _const_spec((1, GMLP_WIDTH))],
        out_specs=out_specs,
        out_shape=out_shape,
        compiler_params=_params(1),
        name="proj",
    )(x2d, g_pre, w_pad, ln_g, ln_b)


def _key_to_float(u):
    bits = jnp.where(u < 0, u ^ jnp.int32(-2 ** 31), ~u)
    return lax.bitcast_convert_type(bits, F32)


def _fold(x, op):
    nq = x.shape[-1]
    part = op(x.reshape(-1, ACC_CHAINS * SUBLANES, nq), axis=0)
    return op(part.reshape(ACC_CHAINS, SUBLANES, nq), axis=0)


def _dsa_kernel(qT_ref, iqT_ref, iwT_ref, ik_ref, k_ref, vT_ref, o_ref,
                score_ref, rscore_ref, widx_ref, wq_ref, s_ref, acc_ref, outT_ref, mt_ref,
                *, nq, causal, n_keys, s_pad, nq_valid, topk):
    j = pl.program_id(1)
    lane = lax.broadcasted_iota(jnp.int32, (1, nq), 1)
    if causal:
        pos = j * nq + lane
        lim = (lax.shift_right_logical(pos, CHUNK.bit_length() - 1) + 1) * CHUNK
        nt = ((j + 1) * nq + (ST - 1)) // ST
        nt_att = ((nt + 1) // 2) * 2
    else:
        lim = jnp.full((1, nq), n_keys, jnp.int32)
        nt = nt_att = s_pad // ST
    neg_inf = jnp.float32(-jnp.inf)
    key_row = lax.broadcasted_iota(jnp.int32, (KT, nq), 0)

    def tile(t):
        return pl.ds(pl.multiple_of(t * ST, ST), ST)

    widx_ref[IDX_DIM:, :] = jnp.zeros((LANES - IDX_DIM, N_IDX_HEADS * nq), BF16)
    for h in range(N_IDX_HEADS):
        widx_ref[:IDX_DIM, h * nq:(h + 1) * nq] = iqT_ref[h * IDX_DIM:(h + 1) * IDX_DIM, :]
    pair_row = lax.broadcasted_iota(jnp.int32, (2 * HEAD_DIM, nq), 0)
    for p in range(N_HEADS // 2):
        blk = qT_ref[p * 2 * HEAD_DIM:(p + 1) * 2 * HEAD_DIM, :]
        zero = jnp.zeros_like(blk)
        wq_ref[p, :, 0:nq] = jnp.where(pair_row < HEAD_DIM, blk, zero)
        wq_ref[p, :, nq:2 * nq] = jnp.where(pair_row >= HEAD_DIM, blk, zero)

    def idx_body(t, carry):
        for sub in range(ST // KT):
            r0 = pl.multiple_of(t * ST + sub * KT, KT)
            logits = jnp.dot(ik_ref[pl.ds(r0, KT), :], widx_ref[...], preferred_element_type=F32)
            sc = jnp.zeros((KT, nq), F32)
            for h in range(N_IDX_HEADS):
                sc = sc + jnp.maximum(logits[:, h * nq:(h + 1) * nq], 0.0) * iwT_ref[h:h + 1, :]
            sc = jnp.where(r0 + key_row < lim, sc, neg_inf)
            score_ref[pl.ds(r0, KT), :] = sc
            rscore_ref[pl.ds(r0, KT), :] = sc.astype(BF16)
        return carry

    lax.fori_loop(0, nt, idx_body, 0)

    def count(indicator):
        def body(t, acc):
            for part in range(ST // COUNT_ROWS):
                r0 = pl.multiple_of(t * ST + part * COUNT_ROWS, COUNT_ROWS)
                acc = acc + _fold(indicator(score_ref[pl.ds(r0, COUNT_ROWS), :]), jnp.sum)
            return acc
        acc = lax.fori_loop(0, nt, body, jnp.zeros((SUBLANES, nq), F32))
        return jnp.sum(acc, axis=0, keepdims=True)

    def count_rounded(cand):
        cb = jnp.broadcast_to(cand, (2 * SUBLANES, nq)).astype(BF16)

        def body(t, acc):
            xb = rscore_ref[tile(t), :].reshape(ST // (2 * SUBLANES), 2 * SUBLANES, nq)
            ind = jnp.where(xb < cb[None], jnp.zeros((), BF16), jnp.ones((), BF16))
            for chain in range(ACC_CHAINS):
                part = ind[chain]
                for r in range(chain + ACC_CHAINS, ind.shape[0], ACC_CHAINS):
                    part = part + ind[r]
                acc = acc + part.astype(F32)
            return acc
        acc = lax.fori_loop(0, nt, body, jnp.zeros((2 * SUBLANES, nq), F32))
        return jnp.sum(acc, axis=0, keepdims=True)

    def bf16_key_to_key(u16):
        return lax.shift_left(u16, 16) | jnp.where(u16 < 2 ** 15, 2 ** 16 - 1, 0)

    def coarse_body(i, u16):
        trial = u16 | lax.shift_left(jnp.int32(1), 15 - i)
        c = count_rounded(_key_to_float(bf16_key_to_key(trial)))
        return jnp.where(c >= topk, trial, u16)

    u16 = lax.fori_loop(0, 16, coarse_body, jnp.zeros((1, nq), jnp.int32))
    base = bf16_key_to_key(u16) - FINE_SPAN // 2

    def fine_step(i, delta, settled):
        trial = delta | lax.shift_left(jnp.int32(1), FINE_BITS - 1 - i)
        cand = _key_to_float(base + trial)
        c = count(lambda x: jnp.where(x < cand, 0.0, 1.0))
        delta = jnp.where(settled > 0.0, delta, jnp.where(c >= topk, trial, delta))
        return delta, jnp.where(c == topk, 1.0, settled)

    def unsettled(settled):
        return jnp.min(settled) == 0.0

    def fine_body(carry):
        i, delta, settled, _ = carry
        delta, settled = fine_step(i, delta, settled)
        delta, settled = fine_step(i + 1, delta, settled)
        return i + 2, delta, settled, unsettled(settled)

    delta, settled = lax.fori_loop(0, FINE_UNTESTED, lambda i, carry: fine_step(i, *carry),
                                   (jnp.zeros((1, nq), jnp.int32), jnp.where(lane >= nq_valid, 1.0, 0.0)))
    _, delta, settled, open_lanes = lax.while_loop(lambda carry: (carry[0] < FINE_BITS) & carry[3], fine_body,
                                                   (jnp.int32(FINE_UNTESTED), delta, settled, unsettled(settled)))
    prefix = base + delta
    thr = jnp.maximum(_key_to_float(prefix), jnp.finfo(F32).min)

    @pl.when(jnp.logical_not(open_lanes))
    def _():
        def body(t, carry):
            score_ref[tile(t), :] = jnp.where(score_ref[tile(t), :] >= thr, 0.0, neg_inf)
            return carry
        lax.fori_loop(0, nt, body, 0)

    @pl.when(open_lanes)
    def _():
        need = topk - count(lambda x: jnp.where(x > thr, 1.0, 0.0))
        r = lax.broadcasted_iota(jnp.int32, (ST, ST), 0)
        c = lax.broadcasted_iota(jnp.int32, (ST, ST), 1)
        lower = jnp.where(c <= r, 1.0, 0.0).astype(BF16)

        def body(t, seen):
            x = score_ref[tile(t), :]
            tied = jnp.where(x == thr, 1.0, 0.0)
            rank = seen + jnp.dot(lower, tied.astype(BF16), preferred_element_type=F32)
            keep = jnp.where(rank <= need, 0.0, neg_inf)
            score_ref[tile(t), :] = jnp.where(x > thr, 0.0, jnp.where(x == thr, keep, neg_inf))
            return rank[ST - 1:ST, :]
        lax.fori_loop(0, nt, body, jnp.zeros((1, nq), F32))

    if causal:
        @pl.when(nt_att > nt)
        def _():
            score_ref[tile(nt), :] = jnp.full((ST, nq), neg_inf, F32)

    acc_ref[...] = jnp.zeros(acc_ref.shape, F32)

    def logits(t, slot):
        for p in range(N_HEADS // 2):
            lp = jnp.dot(k_ref[tile(t), p * 2 * HEAD_DIM:(p + 1) * 2 * HEAD_DIM], wq_ref[p],
                         preferred_element_type=F32)
            for hh in range(2):
                h = 2 * p + hh
                s = lp[:, hh * nq:(hh + 1) * nq] + score_ref[tile(t), :]
                s_ref[slot, h] = s
                mt_ref[slot, h:h + 1, :] = jnp.max(_fold(s, jnp.max), axis=0, keepdims=True)
        return mt_ref[slot]

    def rescale(m_old, mt):
        m_new = jnp.maximum(m_old, mt)
        m_safe = jnp.where(m_new == neg_inf, 0.0, m_new)
        return m_new, m_safe, jnp.exp2(m_old - m_safe)

    def accumulate(t, slot, m_safe, alpha):
        for h in range(N_HEADS):
            rows = slice(h * V_ROWS, (h + 1) * V_ROWS)
            pr = jnp.exp2(s_ref[slot, h] - m_safe[h:h + 1, :]).astype(BF16)
            pv = jnp.dot(vT_ref[rows, tile(t)], pr, preferred_element_type=F32)
            acc_ref[rows, :] = alpha[h:h + 1, :] * acc_ref[rows, :] + pv

    state = rescale(jnp.full((N_HEADS, nq), neg_inf, F32), logits(0, 0))

    def att_body(u, state):
        m, m_safe, alpha = state
        mt = logits(2 * u + 1, 1)
        accumulate(2 * u, 0, m_safe, alpha)
        m, m_safe, alpha = rescale(m, mt)
        mt = logits(2 * u + 2, 0)
        accumulate(2 * u + 1, 1, m_safe, alpha)
        return rescale(m, mt)

    m, m_safe, alpha = lax.fori_loop(0, nt_att // 2 - 1, att_body, state)
    mt = logits(nt_att - 1, 1)
    accumulate(nt_att - 2, 0, m_safe, alpha)
    m, m_safe, alpha = rescale(m, mt)
    accumulate(nt_att - 1, 1, m_safe, alpha)

    for h in range(N_HEADS):
        den = acc_ref[h * V_ROWS + HEAD_DIM:h * V_ROWS + HEAD_DIM + 1, :]
        outT_ref[h * HEAD_DIM:(h + 1) * HEAD_DIM, :] = acc_ref[h * V_ROWS:h * V_ROWS + HEAD_DIM, :] * (1.0 / den)
    o_ref[...] = outT_ref[...].T


def _dsa(qT, iqT, iwT, ikb, kb, vT_aug, *, n_streams, nq, causal, n_keys, nq_valid, topk):
    s_pad = kb.shape[0] // n_streams
    n_qblk = qT.shape[1] // (n_streams * nq)
    kern = functools.partial(_dsa_kernel, nq=nq, causal=causal, n_keys=n_keys, s_pad=s_pad, nq_valid=nq_valid,
                             topk=topk)
    per_q = lambda rows: pl.BlockSpec((rows, nq), lambda bi, j: (0, bi * n_qblk + j))
    one_buf = pl.Buffered(1)
    return pl.pallas_call(
        kern,
        grid=(n_streams, n_qblk),
        in_specs=[per_q(ATTN_WIDTH), per_q(N_IDX_HEADS * IDX_DIM), per_q(N_IDX_HEADS),
                  pl.BlockSpec((s_pad, LANES), lambda bi, j: (bi, 0), pipeline_mode=one_buf),
                  pl.BlockSpec((s_pad, ATTN_WIDTH), lambda bi, j: (bi, 0), pipeline_mode=one_buf),
                  pl.BlockSpec((N_HEADS * V_ROWS, s_pad), lambda bi, j: (0, bi), pipeline_mode=one_buf)],
        out_specs=pl.BlockSpec((nq, ATTN_WIDTH), lambda bi, j: (bi * n_qblk + j, 0)),
        out_shape=jax.ShapeDtypeStruct((n_streams * n_qblk * nq, ATTN_WIDTH), F32),
        scratch_shapes=[pltpu.VMEM((s_pad, nq), F32),
                        pltpu.VMEM((s_pad, nq), BF16),
                        pltpu.VMEM((LANES, N_IDX_HEADS * nq), BF16),
                        pltpu.VMEM((N_HEADS // 2, 2 * HEAD_DIM, 2 * nq), BF16),
                        pltpu.VMEM((2, N_HEADS, ST, nq), F32),
                        pltpu.VMEM((N_HEADS * V_ROWS, nq), F32),
                        pltpu.VMEM((ATTN_WIDTH, nq), F32),
                        pltpu.VMEM((2, N_HEADS, nq), F32)],
        compiler_params=_params(2),
        name="dsa",
    )(qT, iqT, iwT, ikb, kb, vT_aug)


def _mix_out_kernel(x_ref, a_ref, u_ref, vg_ref, ws_ref, bsT_ref, wo_ref, g_ref, y_ref, gated_ref, *, n):
    tm = x_ref.shape[0]
    r = lax.broadcasted_iota(jnp.int32, (n, n), 0)
    c = lax.broadcasted_iota(jnp.int32, (n, n), 1)
    for g in range(GMLP_GROUPS):
        cols = slice(g * GMLP_GROUP_DIM, (g + 1) * GMLP_GROUP_DIM)
        w = jnp.where(c <= r, ws_ref[g], 0.0).astype(BF16)
        bias = bsT_ref[:, g:g + 1]
        for ch in range(tm // n):
            rows = slice(ch * n, (ch + 1) * n)
            s = jnp.dot(w, vg_ref[rows, cols].astype(BF16), preferred_element_type=F32) + bias
            gated_ref[rows, cols] = u_ref[rows, cols] * s
    mix = jnp.dot(a_ref[...].astype(BF16), wo_ref[:ATTN_WIDTH, :], preferred_element_type=F32)
    mix = mix + jnp.dot(gated_ref[...].astype(BF16), wo_ref[ATTN_WIDTH:, :], preferred_element_type=F32)
    y_ref[...] = x_ref[...] + _rms(mix, g_ref[...])


def _mix_out(x2d, attn, u, vg, ws_n, bsT_n, wo_b, g_post, *, n, tm):
    ntok = x2d.shape[0]
    tok = lambda w: pl.BlockSpec((tm, w), lambda i: (i, 0))
    return pl.pallas_call(
        functools.partial(_mix_out_kernel, n=n),
        grid=(ntok // tm,),
        in_specs=[tok(D_MODEL), tok(ATTN_WIDTH), tok(GMLP_WIDTH), tok(GMLP_WIDTH),
                  _const_spec((GMLP_GROUPS, n, n)), _const_spec((n, GMLP_GROUPS)),
                  _const_spec((ATTN_WIDTH + GMLP_WIDTH, D_MODEL)), _const_spec((1, D_MODEL))],
        out_specs=tok(D_MODEL),
        out_shape=jax.ShapeDtypeStruct((ntok, D_MODEL), F32),
        scratch_shapes=[pltpu.VMEM((tm, GMLP_WIDTH), F32)],
        compiler_params=_params(1),
        name="mix_out",
    )(x2d, attn, u, vg, ws_n, bsT_n, wo_b, g_post)


def _mem_kv_kernel(m_ref, g_ref, wk_ref, wv_ref, mk_ref, mv_ref):
    h = _rms(m_ref[...], g_ref[...]).astype(BF16)
    mk_ref[...] = jnp.dot(h, wk_ref[...], preferred_element_type=F32)
    mv_ref[...] = jnp.dot(h, wv_ref[...], preferred_element_type=F32)


def _mem_kv(mem2d, g, wk_b, wv_b):
    n = mem2d.shape[0]
    tm = min(512, n)
    tok = lambda w: pl.BlockSpec((tm, w), lambda i: (i, 0))
    return pl.pallas_call(
        _mem_kv_kernel,
        grid=(n // tm,),
        in_specs=[tok(D_MODEL), _const_spec((1, D_MODEL)), _const_spec((D_MODEL, MEM_WIDTH)),
                  _const_spec((D_MODEL, MEM_WIDTH))],
        out_specs=[tok(MEM_WIDTH), tok(MEM_WIDTH)],
        out_shape=[jax.ShapeDtypeStruct((n, MEM_WIDTH), F32)] * 2,
        compiler_params=_params(1),
        name="mem_kv",
    )(mem2d, g, wk_b, wv_b)


def _mem_attend_kernel(x_ref, mkT_ref, mv_ref, gpre_ref, wq_ref, wo_ref, gpost_ref, y_ref, o_ref):
    x = x_ref[...]
    h = _rms(x, gpre_ref[...]).astype(BF16)
    q = (jnp.dot(h, wq_ref[...], preferred_element_type=F32) * (MEM_HEAD_DIM ** -0.5 * math.log2(math.e))).astype(BF16)
    for hd in range(MEM_HEADS):
        cols = slice(hd * MEM_HEAD_DIM, (hd + 1) * MEM_HEAD_DIM)
        lg = jnp.dot(q[:, cols], mkT_ref[cols, :], preferred_element_type=F32)
        e = jnp.exp2(lg - jnp.max(lg, axis=-1, keepdims=True))
        den = jnp.sum(e, axis=-1, keepdims=True)
        o_ref[:, cols] = jnp.dot(e.astype(BF16), mv_ref[:, cols], preferred_element_type=F32) * (1.0 / den)
    out = jnp.dot(o_ref[...].astype(BF16), wo_ref[...], preferred_element_type=F32)
    y_ref[...] = x + _rms(out, gpost_ref[...])


def _mem_attend(x2d, mkT, mvb, g_pre, wq_b, wo_b, g_post, *, tm, tiles_per_batch):
    ntok = x2d.shape[0]
    n_mem = mvb.shape[1]
    tok = lambda w: pl.BlockSpec((tm, w), lambda i: (i, 0))
    return pl.pallas_call(
        _mem_attend_kernel,
        grid=(ntok // tm,),
        in_specs=[tok(D_MODEL),
                  pl.BlockSpec((None, MEM_WIDTH, n_mem), lambda i: (i // tiles_per_batch, 0, 0)),
                  pl.BlockSpec((None, n_mem, MEM_WIDTH), lambda i: (i // tiles_per_batch, 0, 0)),
                  _const_spec((1, D_MODEL)), _const_spec((D_MODEL, MEM_WIDTH)),
                  _const_spec((MEM_WIDTH, D_MODEL)), _const_spec((1, D_MODEL))],
        out_specs=tok(D_MODEL),
        out_shape=jax.ShapeDtypeStruct((ntok, D_MODEL), F32),
        scratch_shapes=[pltpu.VMEM((tm, MEM_WIDTH), F32)],
        compiler_params=_params(1),
        name="mem_attend",
    )(x2d, mkT, mvb, g_pre, wq_b, wo_b, g_post)


FF_CHUNK = 1024


def _ffn_kernel(x_ref, gpre_ref, w1_ref, w2_ref, gpost_ref, y_ref):
    x = x_ref[...]
    h = _rms(x, gpre_ref[...]).astype(BF16)
    d_ff = w1_ref.shape[1]
    out = jnp.zeros(x.shape, F32)
    for c in range(d_ff // FF_CHUNK):
        cols = slice(c * FF_CHUNK, (c + 1) * FF_CHUNK)
        a = jnp.maximum(jnp.dot(h, w1_ref[:, cols], preferred_element_type=F32), 0.0)
        out = out + jnp.dot((a * a).astype(BF16), w2_ref[cols, :], preferred_element_type=F32)
    y_ref[...] = x + _rms(out, gpost_ref[...])


def _ffn(x2d, g_pre, w1_b, w2_b, g_post):
    ntok = x2d.shape[0]
    tm = min(512, ntok)
    d_ff = w1_b.shape[1]
    tok = pl.BlockSpec((tm, D_MODEL), lambda i: (i, 0))
    return pl.pallas_call(
        _ffn_kernel,
        grid=(ntok // tm,),
        in_specs=[tok, _const_spec((1, D_MODEL)), _const_spec((D_MODEL, d_ff)), _const_spec((d_ff, D_MODEL)),
                  _const_spec((1, D_MODEL))],
        out_specs=tok,
        out_shape=jax.ShapeDtypeStruct((ntok, D_MODEL), F32),
        compiler_params=_params(1),
        name="ffn",
    )(x2d, g_pre, w1_b, w2_b, g_post)


def _pad_axis(a, axis, size):
    pad = [(0, 0)] * a.ndim
    pad[axis] = (0, size - a.shape[axis])
    return jnp.pad(a, pad)


def _augment_vT(vb3):
    b, s, _ = vb3.shape
    vT = jnp.transpose(vb3.reshape(b, s, N_HEADS, HEAD_DIM), (2, 3, 0, 1))
    ones = jnp.ones((N_HEADS, 1, b, s), BF16)
    zeros = jnp.zeros((N_HEADS, V_ROWS - HEAD_DIM - 1, b, s), BF16)
    return jnp.concatenate([vT, ones, zeros], axis=1).reshape(N_HEADS * V_ROWS, b * s)


def _group(x, w, *, past=None, mem_kv=None):
    b, t, _ = x.shape
    x2d = x.reshape(b * t, D_MODEL)
    causal = past is None
    k, kb, v, ikw, ikb, u, vg, *dsa_in = _proj(x2d, w["g_mix_pre"], w["w_pad"], w["ln_g"], w["ln_b"],
                                              transposed=causal)
    ik = ikw[:, :IDX_DIM]
    if causal:
        qT, iqT, iwT, vT_aug = dsa_in
        n_keys = t
        nq = NQ_PROMPT
    else:
        nq = NQ_CHUNK
        q, iq, vb = dsa_in
        ck, cv, cik = past
        n_keys = ck.shape[1] + t
        s_pad = -(-n_keys // (2 * ST)) * (2 * ST)
        tq = -(-t // nq) * nq
        per_stream = lambda a: a.reshape(b, t, a.shape[-1])
        keys = lambda old, new: _pad_axis(jnp.concatenate([old.astype(BF16), per_stream(new)], axis=1), 1, s_pad)
        kb = keys(ck, kb).reshape(b * s_pad, ATTN_WIDTH)
        ikb = keys(_pad_axis(cik, 2, LANES), ikb).reshape(b * s_pad, LANES)
        vT_aug = _augment_vT(keys(cv, vb))
        queries = lambda a: jnp.transpose(_pad_axis(per_stream(a), 1, tq), (2, 0, 1)).reshape(a.shape[-1], b * tq)
        qT, iqT, iwT = queries(q), queries(iq), queries(ikw[:, _IW_LO:_IW_HI])
    topk = min(TOPK_MAX, n_keys // 4)
    attn = _dsa(qT, iqT, iwT, ikb, kb, vT_aug, n_streams=b, nq=nq, causal=causal, n_keys=n_keys,
                nq_valid=min(t, nq), topk=topk)
    attn2d = attn.reshape(b, -1, ATTN_WIDTH)[:, :t].reshape(b * t, ATTN_WIDTH)

    n = min(t, GMLP_CHUNK)
    tm = min(512, b * t)
    x2d = _mix_out(x2d, attn2d, u, vg, w["w_s"][:, :n, :n], jnp.swapaxes(w["b_s"][:, :n], 0, 1),
                   w["w_o"], w["g_mix_post"], n=n, tm=tm)
    mk, mv = mem_kv
    tm_mem = min(512, t)
    x2d = _mem_attend(x2d, jnp.swapaxes(mk, 1, 2).astype(BF16), mv.astype(BF16), w["g_mem_pre"], w["w_mq"],
                      w["w_mo"], w["g_mem_post"], tm=tm_mem, tiles_per_batch=t // tm_mem)
    x2d = _ffn(x2d, w["g_ffn_pre"], w["w1"], w["w2"], w["g_ffn_post"])
    new = (k.reshape(b, t, N_HEADS, HEAD_DIM), v.reshape(b, t, N_HEADS, HEAD_DIM), ik.reshape(b, t, IDX_DIM),
           vg.reshape(b, t, GMLP_GROUPS, GMLP_GROUP_DIM))
    return x2d.reshape(b, t, D_MODEL), new


def kernel(x_prompt, x_sample, cache_attn_k, cache_attn_v, cache_idx_k, cache_mem_k, cache_mem_v, mem_prompt, g_mix_pre, w_in, ln_gmlp_g, ln_gmlp_b, w_s, b_s, w_o, g_mix_post, g_mem_kv, w_mk, w_mv, g_mem_pre, w_mq, w_mo, g_mem_post, g_ffn_pre, w1, w2, g_ffn_post):
    depth = w_in.shape[0]
    xp, xs = x_prompt, x_sample
    bp, n_mem = mem_prompt.shape[0], mem_prompt.shape[1]
    bs = xs.shape[0]
    row = lambda a: a.reshape(1, -1)
    outs = [[] for _ in range(9)]
    for l in range(depth):
        split = _IKW0 + _IW_HI
        w_pad = jnp.concatenate([w_in[l][:, :split], jnp.zeros((D_MODEL, _ZU0 - split), F32), w_in[l][:, split:]],
                                axis=1).astype(BF16)
        w = dict(g_mix_pre=row(g_mix_pre[l]), w_pad=w_pad, ln_g=row(ln_gmlp_g[l]), ln_b=row(ln_gmlp_b[l]),
                 w_s=w_s[l], b_s=b_s[l], w_o=w_o[l].astype(BF16), g_mix_post=row(g_mix_post[l]),
                 g_mem_pre=row(g_mem_pre[l]), w_mq=w_mq[l].astype(BF16), w_mo=w_mo[l].astype(BF16),
                 g_mem_post=row(g_mem_post[l]), g_ffn_pre=row(g_ffn_pre[l]), w1=w1[l].astype(BF16),
                 w2=w2[l].astype(BF16), g_ffn_post=row(g_ffn_post[l]))
        mk, mv = _mem_kv(mem_prompt.reshape(bp * n_mem, D_MODEL), row(g_mem_kv[l]), w_mk[l].astype(BF16),
                         w_mv[l].astype(BF16))
        mk, mv = mk.reshape(bp, n_mem, MEM_WIDTH), mv.reshape(bp, n_mem, MEM_WIDTH)
        xp, (kp, vp, ikp, _) = _group(xp, w, mem_kv=(mk, mv))
        past = (cache_attn_k[l].reshape(bs, -1, ATTN_WIDTH), cache_attn_v[l].reshape(bs, -1, ATTN_WIDTH),
                cache_idx_k[l])
        cmk = cache_mem_k[l].reshape(bs, -1, MEM_WIDTH)
        cmv = cache_mem_v[l].reshape(bs, -1, MEM_WIDTH)
        xs, (ks, vs, iks, gvs) = _group(xs, w, past=past, mem_kv=(cmk, cmv))
        new = (kp, vp, ikp, mk.reshape(bp, n_mem, MEM_HEADS, MEM_HEAD_DIM),
               mv.reshape(bp, n_mem, MEM_HEADS, MEM_HEAD_DIM), ks, vs, iks, gvs)
        for lst, a in zip(outs, new):
            lst.append(a)
    return (xp, xs) + tuple(jnp.stack(lst) for lst in outs)
```

```python
import functools
import math

import jax
import jax.numpy as jnp
from jax import lax
from jax.experimental import pallas as pl
from jax.experimental.pallas import tpu as pltpu

F32 = jnp.float32
BF16 = jnp.bfloat16

D_MODEL = 1024
N_HEADS = 8
HEAD_DIM = 64
ATTN_WIDTH = N_HEADS * HEAD_DIM
N_IDX_HEADS = 8
IDX_DIM = 64
GMLP_GROUPS = 4
GMLP_GROUP_DIM = 128
GMLP_WIDTH = GMLP_GROUPS * GMLP_GROUP_DIM
GMLP_CHUNK = 128
CHUNK = 64
TOPK_MAX = 256
MEM_HEADS = 4
MEM_HEAD_DIM = 128
MEM_WIDTH = MEM_HEADS * MEM_HEAD_DIM
EPS = 1e-6
LN_EPS = 1e-5

LANES = 128
SUBLANES = 8
VMEM_LIMIT = 56 * 1024 * 1024

_Q0, _K0, _V0, _IQ0, _IKW0, _ZU0, _ZV0, _WPAD = 0, 512, 1024, 1536, 2048, 2176, 2688, 3200
_IW_LO, _IW_HI = IDX_DIM, IDX_DIM + N_IDX_HEADS

NQ_PROMPT = 256
NQ_CHUNK = 128
KT = 256
ST = 512
V_ROWS = 80
ACC_CHAINS = 4
COUNT_ROWS = 128
FINE_BITS = 17
FINE_SPAN = 2 ** FINE_BITS
FINE_UNTESTED = 9


def _rms(x, g):
    return x * lax.rsqrt(jnp.mean(x * x, axis=-1, keepdims=True) + EPS) * g


def _params(n_axes):
    return pltpu.CompilerParams(dimension_semantics=("arbitrary",) * n_axes, vmem_limit_bytes=VMEM_LIMIT)


def _const_spec(shape):
    return pl.BlockSpec(shape, lambda *_: (0,) * len(shape), pipeline_mode=pl.Buffered(1))


def _proj_kernel(x_ref, g_ref, w_ref, lng_ref, lnb_ref, *out_refs, transposed):
    k_ref, kb_ref, v_ref, ikw_ref, ikb_ref, u_ref, vg_ref = out_refs[:7]
    h = _rms(x_ref[...], g_ref[...]).astype(BF16)

    def seg(lo, hi):
        return jnp.dot(h, w_ref[:, lo:hi], preferred_element_type=F32)

    q = seg(_Q0, _K0) * (HEAD_DIM ** -0.5 * math.log2(math.e))
    k = seg(_K0, _V0)
    k_ref[...] = k
    kb_ref[...] = k.astype(BF16)
    v = seg(_V0, _IQ0)
    v_ref[...] = v
    iq = seg(_IQ0, _IKW0) * (IDX_DIM ** -0.5)
    ikw = seg(_IKW0, _ZU0)
    lane = lax.broadcasted_iota(jnp.int32, ikw.shape, 1)
    is_iw = (lane >= _IW_LO) & (lane < _IW_HI)
    ikw = jnp.where(is_iw, ikw * (N_IDX_HEADS ** -0.5), ikw)
    ikw_ref[...] = ikw
    ikb_ref[...] = ikw.astype(BF16)
    u_ref[...] = jax.nn.gelu(seg(_ZU0, _ZV0))
    gv = jax.nn.gelu(seg(_ZV0, _WPAD))
    mu = jnp.mean(gv, axis=-1, keepdims=True)
    var = jnp.mean(jnp.square(gv - mu), axis=-1, keepdims=True)
    vg_ref[...] = (gv - mu) * lax.rsqrt(var + LN_EPS) * lng_ref[...] + lnb_ref[...]

    if not transposed:
        q_ref, iq_ref, vb_ref = out_refs[7:]
        q_ref[...] = q.astype(BF16)
        iq_ref[...] = iq.astype(BF16)
        vb_ref[...] = v.astype(BF16)
        return
    qT_ref, iqT_ref, iwT_ref, vTa_ref = out_refs[7:]
    qT_ref[...] = q.T.astype(BF16)
    iqT_ref[...] = iq.T.astype(BF16)
    iwT_ref[...] = ikw.T[_IW_LO:_IW_HI, :]
    vT = v.T
    pad_row = lax.broadcasted_iota(jnp.int32, (V_ROWS - HEAD_DIM, vT.shape[1]), 0)
    ones_then_zeros = jnp.where(pad_row == 0, 1.0, 0.0).astype(BF16)
    for hd in range(N_HEADS):
        vTa_ref[hd * V_ROWS:hd * V_ROWS + HEAD_DIM, :] = vT[hd * HEAD_DIM:(hd + 1) * HEAD_DIM, :].astype(BF16)
        vTa_ref[hd * V_ROWS + HEAD_DIM:(hd + 1) * V_ROWS, :] = ones_then_zeros


def _proj(x2d, g_pre, w_pad, ln_g, ln_b, *, transposed):
    n = x2d.shape[0]
    tm = min(512, n)
    tok = lambda w: pl.BlockSpec((tm, w), lambda i: (i, 0))
    feat = lambda r: pl.BlockSpec((r, tm), lambda i: (0, i))
    outs = [(ATTN_WIDTH, F32), (ATTN_WIDTH, BF16), (ATTN_WIDTH, F32), (LANES, F32), (LANES, BF16),
            (GMLP_WIDTH, F32), (GMLP_WIDTH, F32)]
    out_specs = [tok(w) for w, _ in outs]
    out_shape = [jax.ShapeDtypeStruct((n, w), dt) for w, dt in outs]
    if transposed:
        extra = [(ATTN_WIDTH, BF16), (N_IDX_HEADS * IDX_DIM, BF16), (N_IDX_HEADS, F32), (N_HEADS * V_ROWS, BF16)]
        out_specs += [feat(r) for r, _ in extra]
        out_shape += [jax.ShapeDtypeStruct((r, n), dt) for r, dt in extra]
    else:
        extra = [(ATTN_WIDTH, BF16), (N_IDX_HEADS * IDX_DIM, BF16), (ATTN_WIDTH, BF16)]
        out_specs += [tok(w) for w, _ in extra]
        out_shape += [jax.ShapeDtypeStruct((n, w), dt) for w, dt in extra]
    return pl.pallas_call(
        functools.partial(_proj_kernel, transposed=transposed),
        grid=(n // tm,),
        in_specs=[tok(D_MODEL), _const_spec((1, D_MODEL)), _const_spec((D_MODEL, _WPAD)),
                  _const_spec((1, GMLP_WIDTH)), _const_spec((1, GMLP_WIDTH))],
        out_specs=out_specs,
        out_shape=out_shape,
        compiler_params=_params(1),
        name="proj",
    )(x2d, g_pre, w_pad, ln_g, ln_b)


def _key_to_float(u):
    bits = jnp.where(u < 0, u ^ jnp.int32(-2 ** 31), ~u)
    return lax.bitcast_convert_type(bits, F32)


def _fold(x, op):
    nq = x.shape[-1]
    part = op(x.reshape(-1, ACC_CHAINS * SUBLANES, nq), axis=0)
    return op(part.reshape(ACC_CHAINS, SUBLANES, nq), axis=0)


def _dsa_kernel(qT_ref, iqT_ref, iwT_ref, ik_ref, k_ref, vT_ref, o_ref,
                score_ref, rscore_ref, widx_ref, wq_ref, s_ref, acc_ref, outT_ref, mt_ref,
                *, nq, causal, n_keys, s_pad, nq_valid, topk):
    j = pl.program_id(1)
    lane = lax.broadcasted_iota(jnp.int32, (1, nq), 1)
    if causal:
        pos = j * nq + lane
        lim = (lax.shift_right_logical(pos, CHUNK.bit_length() - 1) + 1) * CHUNK
        nt = ((j + 1) * nq + (ST - 1)) // ST
        nt_att = ((nt + 1) // 2) * 2
    else:
        lim = jnp.full((1, nq), n_keys, jnp.int32)
        nt = nt_att = s_pad // ST
    neg_inf = jnp.float32(-jnp.inf)
    key_row = lax.broadcasted_iota(jnp.int32, (KT, nq), 0)

    def tile(t):
        return pl.ds(pl.multiple_of(t * ST, ST), ST)

    widx_ref[IDX_DIM:, :] = jnp.zeros((LANES - IDX_DIM, N_IDX_HEADS * nq), BF16)
    for h in range(N_IDX_HEADS):
        widx_ref[:IDX_DIM, h * nq:(h + 1) * nq] = iqT_ref[h * IDX_DIM:(h + 1) * IDX_DIM, :]
    pair_row = lax.broadcasted_iota(jnp.int32, (2 * HEAD_DIM, nq), 0)
    for p in range(N_HEADS // 2):
        blk = qT_ref[p * 2 * HEAD_DIM:(p + 1) * 2 * HEAD_DIM, :]
        zero = jnp.zeros_like(blk)
        wq_ref[p, :, 0:nq] = jnp.where(pair_row < HEAD_DIM, blk, zero)
        wq_ref[p, :, nq:2 * nq] = jnp.where(pair_row >= HEAD_DIM, blk, zero)

    def idx_body(t, carry):
        for sub in range(ST // KT):
            r0 = pl.multiple_of(t * ST + sub * KT, KT)
            logits = jnp.dot(ik_ref[pl.ds(r0, KT), :], widx_ref[...], preferred_element_type=F32)
            sc = jnp.zeros((KT, nq), F32)
            for h in range(N_IDX_HEADS):
                sc = sc + jnp.maximum(logits[:, h * nq:(h + 1) * nq], 0.0) * iwT_ref[h:h + 1, :]
            sc = jnp.where(r0 + key_row < lim, sc, neg_inf)
            score_ref[pl.ds(r0, KT), :] = sc
            rscore_ref[pl.ds(r0, KT), :] = sc.astype(BF16)
        return carry

    lax.fori_loop(0, nt, idx_body, 0)

    def count(indicator):
        def body(t, acc):
            for part in range(ST // COUNT_ROWS):
                r0 = pl.multiple_of(t * ST + part * COUNT_ROWS, COUNT_ROWS)
                acc = acc + _fold(indicator(score_ref[pl.ds(r0, COUNT_ROWS), :]), jnp.sum)
            return acc
        acc = lax.fori_loop(0, nt, body, jnp.zeros((SUBLANES, nq), F32))
        return jnp.sum(acc, axis=0, keepdims=True)

    def count_rounded(cand):
        cb = jnp.broadcast_to(cand, (2 * SUBLANES, nq)).astype(BF16)

        def body(t, acc):
            xb = rscore_ref[tile(t), :].reshape(ST // (2 * SUBLANES), 2 * SUBLANES, nq)
            ind = jnp.where(xb < cb[None], jnp.zeros((), BF16), jnp.ones((), BF16))
            for chain in range(ACC_CHAINS):
                part = ind[chain]
                for r in range(chain + ACC_CHAINS, ind.shape[0], ACC_CHAINS):
                    part = part + ind[r]
                acc = acc + part.astype(F32)
            return acc
        acc = lax.fori_loop(0, nt, body, jnp.zeros((2 * SUBLANES, nq), F32))
        return jnp.sum(acc, axis=0, keepdims=True)

    def bf16_key_to_key(u16):
        return lax.shift_left(u16, 16) | jnp.where(u16 < 2 ** 15, 2 ** 16 - 1, 0)

    def coarse_body(i, u16):
        trial = u16 | lax.shift_left(jnp.int32(1), 15 - i)
        c = count_rounded(_key_to_float(bf16_key_to_key(trial)))
        return jnp.where(c >= topk, trial, u16)

    u16 = lax.fori_loop(0, 16, coarse_body, jnp.zeros((1, nq), jnp.int32))
    base = bf16_key_to_key(u16) - FINE_SPAN // 2

    def fine_step(i, delta, settled):
        trial = delta | lax.shift_left(jnp.int32(1), FINE_BITS - 1 - i)
        cand = _key_to_float(base + trial)
        c = count(lambda x: jnp.where(x < cand, 0.0, 1.0))
        delta = jnp.where(settled > 0.0, delta, jnp.where(c >= topk, trial, delta))
        return delta, jnp.where(c == topk, 1.0, settled)

    def unsettled(settled):
        return jnp.min(settled) == 0.0

    def fine_body(carry):
        i, delta, settled, _ = carry
        delta, settled = fine_step(i, delta, settled)
        delta, settled = fine_step(i + 1, delta, settled)
        return i + 2, delta, settled, unsettled(settled)

    delta, settled = lax.fori_loop(0, FINE_UNTESTED, lambda i, carry: fine_step(i, *carry),
                                   (jnp.zeros((1, nq), jnp.int32), jnp.where(lane >= nq_valid, 1.0, 0.0)))
    _, delta, settled, open_lanes = lax.while_loop(lambda carry: (carry[0] < FINE_BITS) & carry[3], fine_body,
                                                   (jnp.int32(FINE_UNTESTED), delta, settled, unsettled(settled)))
    prefix = base + delta
    thr = jnp.maximum(_key_to_float(prefix), jnp.finfo(F32).min)

    @pl.when(jnp.logical_not(open_lanes))
    def _():
        def body(t, carry):
            score_ref[tile(t), :] = jnp.where(score_ref[tile(t), :] >= thr, 0.0, neg_inf)
            return carry
        lax.fori_loop(0, nt, body, 0)

    @pl.when(open_lanes)
    def _():
        need = topk - count(lambda x: jnp.where(x > thr, 1.0, 0.0))
        r = lax.broadcasted_iota(jnp.int32, (ST, ST), 0)
        c = lax.broadcasted_iota(jnp.int32, (ST, ST), 1)
        lower = jnp.where(c <= r, 1.0, 0.0).astype(BF16)

        def body(t, seen):
            x = score_ref[tile(t), :]
            tied = jnp.where(x == thr, 1.0, 0.0)
            rank = seen + jnp.dot(lower, tied.astype(BF16), preferred_element_type=F32)
            keep = jnp.where(rank <= need, 0.0, neg_inf)
            score_ref[tile(t), :] = jnp.where(x > thr, 0.0, jnp.where(x == thr, keep, neg_inf))
            return rank[ST - 1:ST, :]
        lax.fori_loop(0, nt, body, jnp.zeros((1, nq), F32))

    if causal:
        @pl.when(nt_att > nt)
        def _():
            score_ref[tile(nt), :] = jnp.full((ST, nq), neg_inf, F32)

    acc_ref[...] = jnp.zeros(acc_ref.shape, F32)

    def logits(t, slot):
        for p in range(N_HEADS // 2):
            lp = jnp.dot(k_ref[tile(t), p * 2 * HEAD_DIM:(p + 1) * 2 * HEAD_DIM], wq_ref[p],
                         preferred_element_type=F32)
            for hh in range(2):
                h = 2 * p + hh
                s = lp[:, hh * nq:(hh + 1) * nq] + score_ref[tile(t), :]
                s_ref[slot, h] = s
                mt_ref[slot, h:h + 1, :] = jnp.max(_fold(s, jnp.max), axis=0, keepdims=True)
        return mt_ref[slot]

    def rescale(m_old, mt):
        m_new = jnp.maximum(m_old, mt)
        m_safe = jnp.where(m_new == neg_inf, 0.0, m_new)
        return m_new, m_safe, jnp.exp2(m_old - m_safe)

    def accumulate(t, slot, m_safe, alpha):
        for h in range(N_HEADS):
            rows = slice(h * V_ROWS, (h + 1) * V_ROWS)
            pr = jnp.exp2(s_ref[slot, h] - m_safe[h:h + 1, :]).astype(BF16)
            pv = jnp.dot(vT_ref[rows, tile(t)], pr, preferred_element_type=F32)
            acc_ref[rows, :] = alpha[h:h + 1, :] * acc_ref[rows, :] + pv

    state = rescale(jnp.full((N_HEADS, nq), neg_inf, F32), logits(0, 0))

    def att_body(u, state):
        m, m_safe, alpha = state
        mt = logits(2 * u + 1, 1)
        accumulate(2 * u, 0, m_safe, alpha)
        m, m_safe, alpha = rescale(m, mt)
        mt = logits(2 * u + 2, 0)
        accumulate(2 * u + 1, 1, m_safe, alpha)
        return rescale(m, mt)

    m, m_safe, alpha = lax.fori_loop(0, nt_att // 2 - 1, att_body, state)
    mt = logits(nt_att - 1, 1)
    accumulate(nt_att - 2, 0, m_safe, alpha)
    m, m_safe, alpha = rescale(m, mt)
    accumulate(nt_att - 1, 1, m_safe, alpha)

    for h in range(N_HEADS):
        den = acc_ref[h * V_ROWS + HEAD_DIM:h * V_ROWS + HEAD_DIM + 1, :]
        outT_ref[h * HEAD_DIM:(h + 1) * HEAD_DIM, :] = acc_ref[h * V_ROWS:h * V_ROWS + HEAD_DIM, :] * (1.0 / den)
    o_ref[...] = outT_ref[...].T


def _dsa(qT, iqT, iwT, ikb, kb, vT_aug, *, n_streams, nq, causal, n_keys, nq_valid, topk):
    s_pad = kb.shape[0] // n_streams
    n_qblk = qT.shape[1] // (n_streams * nq)
    kern = functools.partial(_dsa_kernel, nq=nq, causal=causal, n_keys=n_keys, s_pad=s_pad, nq_valid=nq_valid,
                             topk=topk)
    per_q = lambda rows: pl.BlockSpec((rows, nq), lambda bi, j: (0, bi * n_qblk + j))
    one_buf = pl.Buffered(1)
    return pl.pallas_call(
        kern,
        grid=(n_streams, n_qblk),
        in_specs=[per_q(ATTN_WIDTH), per_q(N_IDX_HEADS * IDX_DIM), per_q(N_IDX_HEADS),
                  pl.BlockSpec((s_pad, LANES), lambda bi, j: (bi, 0), pipeline_mode=one_buf),
                  pl.BlockSpec((s_pad, ATTN_WIDTH), lambda bi, j: (bi, 0), pipeline_mode=one_buf),
                  pl.BlockSpec((N_HEADS * V_ROWS, s_pad), lambda bi, j: (0, bi), pipeline_mode=one_buf)],
        out_specs=pl.BlockSpec((nq, ATTN_WIDTH), lambda bi, j: (bi * n_qblk + j, 0)),
        out_shape=jax.ShapeDtypeStruct((n_streams * n_qblk * nq, ATTN_WIDTH), F32),
        scratch_shapes=[pltpu.VMEM((s_pad, nq), F32),
                        pltpu.VMEM((s_pad, nq), BF16),
                        pltpu.VMEM((LANES, N_IDX_HEADS * nq), BF16),
                        pltpu.VMEM((N_HEADS // 2, 2 * HEAD_DIM, 2 * nq), BF16),
                        pltpu.VMEM((2, N_HEADS, ST, nq), F32),
                        pltpu.VMEM((N_HEADS * V_ROWS, nq), F32),
                        pltpu.VMEM((ATTN_WIDTH, nq), F32),
                        pltpu.VMEM((2, N_HEADS, nq), F32)],
        compiler_params=_params(2),
        name="dsa",
    )(qT, iqT, iwT, ikb, kb, vT_aug)


def _mix_mem_kernel(x_ref, a_ref, u_ref, vg_ref, ws_ref, bsT_ref, wo_ref, g_ref,
                    mkT_ref, mv_ref, gpre_ref, wq_ref, wmo_ref, gpost_ref, y_ref, gated_ref, o_ref, *, n):
    tm = x_ref.shape[0]
    r = lax.broadcasted_iota(jnp.int32, (n, n), 0)
    c = lax.broadcasted_iota(jnp.int32, (n, n), 1)
    for g in range(GMLP_GROUPS):
        cols = slice(g * GMLP_GROUP_DIM, (g + 1) * GMLP_GROUP_DIM)
        w = jnp.where(c <= r, ws_ref[g], 0.0).astype(BF16)
        bias = bsT_ref[:, g:g + 1]
        for ch in range(tm // n):
            rows = slice(ch * n, (ch + 1) * n)
            s = jnp.dot(w, vg_ref[rows, cols].astype(BF16), preferred_element_type=F32) + bias
            gated_ref[rows, cols] = u_ref[rows, cols] * s
    mix = jnp.dot(a_ref[...].astype(BF16), wo_ref[:ATTN_WIDTH, :], preferred_element_type=F32)
    mix = mix + jnp.dot(gated_ref[...].astype(BF16), wo_ref[ATTN_WIDTH:, :], preferred_element_type=F32)
    x = x_ref[...] + _rms(mix, g_ref[...])

    h = _rms(x, gpre_ref[...]).astype(BF16)
    q = (jnp.dot(h, wq_ref[...], preferred_element_type=F32) * (MEM_HEAD_DIM ** -0.5 * math.log2(math.e))).astype(BF16)
    for hd in range(MEM_HEADS):
        cols = slice(hd * MEM_HEAD_DIM, (hd + 1) * MEM_HEAD_DIM)
        lg = jnp.dot(q[:, cols], mkT_ref[cols, :], preferred_element_type=F32)
        e = jnp.exp2(lg - jnp.max(lg, axis=-1, keepdims=True))
        den = jnp.sum(e, axis=-1, keepdims=True)
        o_ref[:, cols] = jnp.dot(e.astype(BF16), mv_ref[:, cols], preferred_element_type=F32) * (1.0 / den)
    out = jnp.dot(o_ref[...].astype(BF16), wmo_ref[...], preferred_element_type=F32)
    y_ref[...] = x + _rms(out, gpost_ref[...])


def _mix_mem(x2d, attn, u, vg, ws_n, bsT_n, wo_b, g_post, mkT, mvb, g_pre, wq_b, wmo_b, g_mem_post,
             *, n, tm, tiles_per_batch):
    ntok = x2d.shape[0]
    n_mem = mvb.shape[1]
    tok = lambda w: pl.BlockSpec((tm, w), lambda i: (i, 0))
    return pl.pallas_call(
        functools.partial(_mix_mem_kernel, n=n),
        grid=(ntok // tm,),
        in_specs=[tok(D_MODEL), tok(ATTN_WIDTH), tok(GMLP_WIDTH), tok(GMLP_WIDTH),
                  _const_spec((GMLP_GROUPS, n, n)), _const_spec((n, GMLP_GROUPS)),
                  _const_spec((ATTN_WIDTH + GMLP_WIDTH, D_MODEL)), _const_spec((1, D_MODEL)),
                  pl.BlockSpec((None, MEM_WIDTH, n_mem), lambda i: (i // tiles_per_batch, 0, 0)),
                  pl.BlockSpec((None, n_mem, MEM_WIDTH), lambda i: (i // tiles_per_batch, 0, 0)),
                  _const_spec((1, D_MODEL)), _const_spec((D_MODEL, MEM_WIDTH)),
                  _const_spec((MEM_WIDTH, D_MODEL)), _const_spec((1, D_MODEL))],
        out_specs=tok(D_MODEL),
        out_shape=jax.ShapeDtypeStruct((ntok, D_MODEL), F32),
        scratch_shapes=[pltpu.VMEM((tm, GMLP_WIDTH), F32), pltpu.VMEM((tm, MEM_WIDTH), F32)],
        compiler_params=_params(1),
        name="mix_mem",
    )(x2d, attn, u, vg, ws_n, bsT_n, wo_b, g_post, mkT, mvb, g_pre, wq_b, wmo_b, g_mem_post)


def _mem_kv_kernel(m_ref, g_ref, wk_ref, wv_ref, mk_ref, mv_ref):
    h = _rms(m_ref[...], g_ref[...]).astype(BF16)
    mk_ref[...] = jnp.dot(h, wk_ref[...], preferred_element_type=F32)
    mv_ref[...] = jnp.dot(h, wv_ref[...], preferred_element_type=F32)


def _mem_kv(mem2d, g, wk_b, wv_b):
    n = mem2d.shape[0]
    tm = min(512, n)
    tok = lambda w: pl.BlockSpec((tm, w), lambda i: (i, 0))
    return pl.pallas_call(
        _mem_kv_kernel,
        grid=(n // tm,),
        in_specs=[tok(D_MODEL), _const_spec((1, D_MODEL)), _const_spec((D_MODEL, MEM_WIDTH)),
                  _const_spec((D_MODEL, MEM_WIDTH))],
        out_specs=[tok(MEM_WIDTH), tok(MEM_WIDTH)],
        out_shape=[jax.ShapeDtypeStruct((n, MEM_WIDTH), F32)] * 2,
        compiler_params=_params(1),
        name="mem_kv",
    )(mem2d, g, wk_b, wv_b)


FF_CHUNK = 1024


def _ffn_kernel(x_ref, gpre_ref, w1_ref, w2_ref, gpost_ref, y_ref):
    x = x_ref[...]
    h = _rms(x, gpre_ref[...]).astype(BF16)
    d_ff = w1_ref.shape[1]
    out = jnp.zeros(x.shape, F32)
    for c in range(d_ff // FF_CHUNK):
        cols = slice(c * FF_CHUNK, (c + 1) * FF_CHUNK)
        a = jnp.maximum(jnp.dot(h, w1_ref[:, cols], preferred_element_type=F32), 0.0)
        out = out + jnp.dot((a * a).astype(BF16), w2_ref[cols, :], preferred_element_type=F32)
    y_ref[...] = x + _rms(out, gpost_ref[...])


def _ffn(x2d, g_pre, w1_b, w2_b, g_post):
    ntok = x2d.shape[0]
    tm = min(512, ntok)
    d_ff = w1_b.shape[1]
    tok = pl.BlockSpec((tm, D_MODEL), lambda i: (i, 0))
    return pl.pallas_call(
        _ffn_kernel,
        grid=(ntok // tm,),
        in_specs=[tok, _const_spec((1, D_MODEL)), _const_spec((D_MODEL, d_ff)), _const_spec((d_ff, D_MODEL)),
                  _const_spec((1, D_MODEL))],
        out_specs=tok,
        out_shape=jax.ShapeDtypeStruct((ntok, D_MODEL), F32),
        compiler_params=_params(1),
        name="ffn",
    )(x2d, g_pre, w1_b, w2_b, g_post)


def _pad_axis(a, axis, size):
    pad = [(0, 0)] * a.ndim
    pad[axis] = (0, size - a.shape[axis])
    return jnp.pad(a, pad)


def _augment_vT(vb3):
    b, s, _ = vb3.shape
    vT = jnp.transpose(vb3.reshape(b, s, N_HEADS, HEAD_DIM), (2, 3, 0, 1))
    ones = jnp.ones((N_HEADS, 1, b, s), BF16)
    zeros = jnp.zeros((N_HEADS, V_ROWS - HEAD_DIM - 1, b, s), BF16)
    return jnp.concatenate([vT, ones, zeros], axis=1).reshape(N_HEADS * V_ROWS, b * s)


def _group(x, w, *, past=None, mem_kv=None):
    b, t, _ = x.shape
    x2d = x.reshape(b * t, D_MODEL)
    causal = past is None
    k, kb, v, ikw, ikb, u, vg, *dsa_in = _proj(x2d, w["g_mix_pre"], w["w_pad"], w["ln_g"], w["ln_b"],
                                              transposed=causal)
    ik = ikw[:, :IDX_DIM]
    if causal:
        qT, iqT, iwT, vT_aug = dsa_in
        n_keys = t
        nq = NQ_PROMPT
    else:
        nq = NQ_CHUNK
        q, iq, vb = dsa_in
        ck, cv, cik = past
        n_keys = ck.shape[1] + t
        s_pad = -(-n_keys // (2 * ST)) * (2 * ST)
        tq = -(-t // nq) * nq
        per_stream = lambda a: a.reshape(b, t, a.shape[-1])
        keys = lambda old, new: _pad_axis(jnp.concatenate([old.astype(BF16), per_stream(new)], axis=1), 1, s_pad)
        kb = keys(ck, kb).reshape(b * s_pad, ATTN_WIDTH)
        ikb = keys(_pad_axis(cik, 2, LANES), ikb).reshape(b * s_pad, LANES)
        vT_aug = _augment_vT(keys(cv, vb))
        queries = lambda a: jnp.transpose(_pad_axis(per_stream(a), 1, tq), (2, 0, 1)).reshape(a.shape[-1], b * tq)
        qT, iqT, iwT = queries(q), queries(iq), queries(ikw[:, _IW_LO:_IW_HI])
    topk = min(TOPK_MAX, n_keys // 4)
    attn = _dsa(qT, iqT, iwT, ikb, kb, vT_aug, n_streams=b, nq=nq, causal=causal, n_keys=n_keys,
                nq_valid=min(t, nq), topk=topk)
    attn2d = attn.reshape(b, -1, ATTN_WIDTH)[:, :t].reshape(b * t, ATTN_WIDTH)

    n = min(t, GMLP_CHUNK)
    tm = min(512, t)
    mk, mv = mem_kv
    x2d = _mix_mem(x2d, attn2d, u, vg, w["w_s"][:, :n, :n], jnp.swapaxes(w["b_s"][:, :n], 0, 1),
                   w["w_o"], w["g_mix_post"], jnp.swapaxes(mk, 1, 2).astype(BF16), mv.astype(BF16),
                   w["g_mem_pre"], w["w_mq"], w["w_mo"], w["g_mem_post"], n=n, tm=tm, tiles_per_batch=t // tm)
    x2d = _ffn(x2d, w["g_ffn_pre"], w["w1"], w["w2"], w["g_ffn_post"])
    new = (k.reshape(b, t, N_HEADS, HEAD_DIM), v.reshape(b, t, N_HEADS, HEAD_DIM), ik.reshape(b, t, IDX_DIM),
           vg.reshape(b, t, GMLP_GROUPS, GMLP_GROUP_DIM))
    return x2d.reshape(b, t, D_MODEL), new


def kernel(x_prompt, x_sample, cache_attn_k, cache_attn_v, cache_idx_k, cache_mem_k, cache_mem_v, mem_prompt, g_mix_pre, w_in, ln_gmlp_g, ln_gmlp_b, w_s, b_s, w_o, g_mix_post, g_mem_kv, w_mk, w_mv, g_mem_pre, w_mq, w_mo, g_mem_post, g_ffn_pre, w1, w2, g_ffn_post):
    depth = w_in.shape[0]
    xp, xs = x_prompt, x_sample
    bp, n_mem = mem_prompt.shape[0], mem_prompt.shape[1]
    bs = xs.shape[0]
    row = lambda a: a.reshape(1, -1)
    outs = [[] for _ in range(9)]
    for l in range(depth):
        split = _IKW0 + _IW_HI
        w_pad = jnp.concatenate([w_in[l][:, :split], jnp.zeros((D_MODEL, _ZU0 - split), F32), w_in[l][:, split:]],
                                axis=1).astype(BF16)
        w = dict(g_mix_pre=row(g_mix_pre[l]), w_pad=w_pad, ln_g=row(ln_gmlp_g[l]), ln_b=row(ln_gmlp_b[l]),
                 w_s=w_s[l], b_s=b_s[l], w_o=w_o[l].astype(BF16), g_mix_post=row(g_mix_post[l]),
                 g_mem_pre=row(g_mem_pre[l]), w_mq=w_mq[l].astype(BF16), w_mo=w_mo[l].astype(BF16),
                 g_mem_post=row(g_mem_post[l]), g_ffn_pre=row(g_ffn_pre[l]), w1=w1[l].astype(BF16),
                 w2=w2[l].astype(BF16), g_ffn_post=row(g_ffn_post[l]))
        mk, mv = _mem_kv(mem_prompt.reshape(bp * n_mem, D_MODEL), row(g_mem_kv[l]), w_mk[l].astype(BF16),
                         w_mv[l].astype(BF16))
        mk, mv = mk.reshape(bp, n_mem, MEM_WIDTH), mv.reshape(bp, n_mem, MEM_WIDTH)
        xp, (kp, vp, ikp, _) = _group(xp, w, mem_kv=(mk, mv))
        past = (cache_attn_k[l].reshape(bs, -1, ATTN_WIDTH), cache_attn_v[l].reshape(bs, -1, ATTN_WIDTH),
                cache_idx_k[l])
        cmk = cache_mem_k[l].reshape(bs, -1, MEM_WIDTH)
        cmv = cache_mem_v[l].reshape(bs, -1, MEM_WIDTH)
        xs, (ks, vs, iks, gvs) = _group(xs, w, past=past, mem_kv=(cmk, cmv))
        new = (kp, vp, ikp, mk.reshape(bp, n_mem, MEM_HEADS, MEM_HEAD_DIM),
               mv.reshape(bp, n_mem, MEM_HEADS, MEM_HEAD_DIM), ks, vs, iks, gvs)
        for lst, a in zip(outs, new):
            lst.append(a)
    return (xp, xs) + tuple(jnp.stack(lst) for lst in outs)
```
